```python
import math
import jax
import jax.numpy as jnp
from jax import lax
import numpy as np

D_MODEL = 1024
BATCH = 2
SEQ = 16384
DEPTH = 4

GRID_W = 64
CTX_LEN = 256
D_MIX = D_MODEL
HEAD_DIM = 64
D_HY = D_MIX // 4
D_LRU = D_MIX // 4
N_HEADS_WIN = (D_MIX // 4) // HEAD_DIM
N_KV_WIN = N_HEADS_WIN // 2
N_HEADS_GLB = (D_MIX // 4) // HEAD_DIM
N_KV_GLB = N_HEADS_GLB // 2
D_FF = 2816
N_MOD = 9
WINDOW = 128
Q_BLOCK = 128
HY_ORDER = 2
HY_SHORT = 3
HY_PAD = (1, 1)
LRU_SHORT = 4
LRU_PAD = (2, 1)
HY_BANDS = 16
HY_EMB = 1 + 2 * HY_BANDS
HY_HIDDEN = 64
HY_MIN_DECAY = math.log(1e-2) / 1.5
HY_MAX_DECAY = math.log(1e-2) / 0.3
LRU_BLOCKS = 4
LRU_BLOCK_DIM = D_LRU // LRU_BLOCKS
LRU_C = 8.0
ROPE_BASE = 10000.0
ROPE_FREQS = HEAD_DIM // 4
EPS = 1e-6
NEG_INF = -1e30
SPLIT_SIZES = (3 * D_HY, D_LRU, D_LRU,
               N_HEADS_WIN * HEAD_DIM, N_KV_WIN * HEAD_DIM, N_KV_WIN * HEAD_DIM,
               N_HEADS_GLB * HEAD_DIM, N_KV_GLB * HEAD_DIM, N_KV_GLB * HEAD_DIM)
SPLIT_IDX = tuple(int(i) for i in np.cumsum(SPLIT_SIZES)[:-1])
D_IN = sum(SPLIT_SIZES)
F32 = jnp.float32

kernel_name = 'hybrid_headgroup_diffusion_trunk'


def rmsnorm(x, g):
    x32 = x.astype(F32)
    y = x32 * lax.rsqrt(jnp.mean(x32 * x32, axis=-1, keepdims=True) + EPS)
    return (y * g.astype(F32)).astype(x.dtype)


def adaln_in(h, g, shift, scale):
    return rmsnorm(h, g) * (1 + scale) + shift


def swiglu(h, w13, w2):
    gate, up = jnp.split(h @ w13, 2, axis=-1)
    return (jax.nn.silu(gate) * up) @ w2


def depthwise_conv(u, w, b, pad):
    y = lax.conv_general_dilated(u, w[:, None, :].astype(u.dtype), window_strides=(1,), padding=[pad],
                                 dimension_numbers=('NWC', 'WIO', 'NWC'), feature_group_count=u.shape[-1])
    return y + b


def heads(z):
    return z.reshape(*z.shape[:-1], -1, HEAD_DIM)


def axial_rope(n_tokens):
    rows = n_tokens // GRID_W
    row = jnp.repeat(jnp.arange(rows, dtype=F32), GRID_W, total_repeat_length=n_tokens)
    col = jnp.tile(jnp.arange(GRID_W, dtype=F32), rows)
    inv = ROPE_BASE ** (-jnp.arange(ROPE_FREQS, dtype=F32) / ROPE_FREQS)
    ang = jnp.concatenate([row[:, None] * inv, col[:, None] * inv], axis=-1)
    return jnp.cos(ang), jnp.sin(ang)


def apply_rope(z, cos, sin):
    z32 = z.astype(F32)
    z1, z2 = jnp.split(z32, 2, axis=-1)
    c, s = cos[:, None, :], sin[:, None, :]
    return jnp.concatenate([z1 * c - z2 * s, z1 * s + z2 * c], axis=-1).astype(z.dtype)


def hyena_filter_spectrum(n, w1, b1, freq, w2, b2, w3):
    t = jnp.linspace(0.0, 1.0, n, dtype=F32)[:, None]
    w = 2.0 * math.pi * jnp.arange(n, dtype=F32)[:, None] / n
    f = jnp.linspace(1e-4, HY_BANDS - 1, HY_BANDS, dtype=F32)[None, :]
    z = jnp.concatenate([t, jnp.cos(f * w), -jnp.sin(f * w)], axis=-1)
    h = jnp.sin(freq[0] * (z @ w1 + b1))
    h = jnp.sin(freq[1] * (h @ w2 + b2))
    h = (h @ w3).astype(F32).reshape(n, HY_ORDER, 2, D_HY)
    deltas = jnp.abs(jnp.linspace(HY_MIN_DECAY, HY_MAX_DECAY, D_HY, dtype=F32))
    h = h * jnp.exp(-t[:, :, None, None] * deltas)
    h = h / (jnp.sum(jnp.abs(h), axis=(0, 2), keepdims=True) + EPS)
    fwd, bwd = h[:, :, 0], h[:, :, 1]
    k = jnp.concatenate([fwd, jnp.zeros_like(fwd[:1]), bwd[:0:-1]], axis=0)
    return jnp.fft.rfft(k, axis=0)


def hyena_mixer(u, kf, skip):
    n = u.shape[1]
    v, x1, x2 = jnp.split(u, 3, axis=-1)

    def long_conv(z, k_f, s):
        z32 = z.astype(F32)
        zf = jnp.fft.rfft(z32, n=2 * n, axis=1)
        y = jnp.fft.irfft(zf * k_f[None], n=2 * n, axis=1)[:, :n]
        return (y + z32 * s.astype(F32)).astype(z.dtype)

    z = x1 * long_conv(v, kf[:, 0], skip[0])
    return x2 * long_conv(z, kf[:, 1], skip[1])


def block_diag(z, w, b):
    zb = z.reshape(*z.shape[:-1], LRU_BLOCKS, LRU_BLOCK_DIM)
    return jnp.einsum('blnd,nde->blne', zb, w).reshape(z.shape) + b


def rglru_coeffs(u, wa, ba, wx, bx, lam):
    r = jax.nn.sigmoid(block_diag(u, wa, ba).astype(F32))
    i = jax.nn.sigmoid(block_diag(u, wx, bx).astype(F32))
    log_a = -LRU_C * r * jax.nn.softplus(-lam.astype(F32))
    return jnp.exp(log_a), jnp.sqrt(-jnp.expm1(2.0 * log_a)) * i * u.astype(F32)


def linear_scan(a, b, h0=None):
    if h0 is not None:
        b = b.at[:, 0].add(a[:, 0] * h0)

    def combine(lft, rgt):
        return (lft[0] * rgt[0], rgt[0] * lft[1] + rgt[1])

    return lax.associative_scan(combine, (a, b), axis=1)[1]


def flip_seq(z, d):
    return z[:, ::-1] if d == 1 else z


def rglru_bidirectional(ul, uc, wa, ba, wx, bx, lam):
    yl, yc = 0.0, 0.0
    for d in range(2):
        a_c, b_c = rglru_coeffs(flip_seq(uc, d), wa[d], ba[d], wx[d], bx[d], lam[d])
        h_c = linear_scan(a_c, b_c)
        a_l, b_l = rglru_coeffs(flip_seq(ul, d), wa[d], ba[d], wx[d], bx[d], lam[d])
        h_l = linear_scan(a_l, b_l, h_c[:, -1])
        yl = yl + flip_seq(h_l, d)
        yc = yc + flip_seq(h_c, d)
    return yl, yc


def dense_attention(q, k, v, sink):
    B, n, H, dh = q.shape
    KVH, m = k.shape[2], k.shape[1]
    G = H // KVH
    s = jnp.einsum('bqkgd,bskd->bkgqs', q.reshape(B, n, KVH, G, dh), k, preferred_element_type=F32) * dh ** -0.5
    if sink is not None:
        s = jnp.concatenate([s, jnp.broadcast_to(sink.astype(F32).reshape(1, KVH, G, 1, 1), s.shape[:-1] + (1,))], axis=-1)
    p = jax.nn.softmax(s, axis=-1)[..., :m].astype(v.dtype)
    return jnp.einsum('bkgqs,bskd->bqkgd', p, v).reshape(B, n, H * dh)


def window_attention(q, k, v, kc, vc, sink):
    B, n, H, dh = q.shape
    KVH = k.shape[2]
    G = H // KVH
    C = kc.shape[1]
    nb = n // Q_BLOCK
    qb = q.reshape(B, nb, Q_BLOCK, KVH, G, dh)

    def band(z):
        zp = jnp.pad(z, ((0, 0), (Q_BLOCK, Q_BLOCK), (0, 0), (0, 0))).reshape(B, nb + 2, Q_BLOCK, KVH, dh)
        return jnp.concatenate([zp[:, :-2], zp[:, 1:-1], zp[:, 2:]], axis=2)

    kb, vb = band(k), band(v)
    scale = dh ** -0.5
    s_loc = jnp.einsum('bnqkgd,bnskd->bnkgqs', qb, kb, preferred_element_type=F32) * scale
    s_ctx = jnp.einsum('bnqkgd,bskd->bnkgqs', qb, kc, preferred_element_type=F32) * scale
    qpos = jnp.arange(nb)[:, None] * Q_BLOCK + jnp.arange(Q_BLOCK)[None, :]
    kpos = (jnp.arange(nb)[:, None] - 1) * Q_BLOCK + jnp.arange(3 * Q_BLOCK)[None, :]
    kp = kpos[:, None, :]
    valid = (jnp.abs(kp - qpos[:, :, None]) <= WINDOW) & (kp >= 0) & (kp < n)
    s_loc = jnp.where(valid[None, :, None, None], s_loc, NEG_INF)
    s_sink = jnp.broadcast_to(sink.astype(F32).reshape(1, 1, KVH, G, 1, 1), s_loc.shape[:-1] + (1,))
    p = jax.nn.softmax(jnp.concatenate([s_loc, s_ctx, s_sink], axis=-1), axis=-1).astype(v.dtype)
    n_loc = 3 * Q_BLOCK
    o = (jnp.einsum('bnkgqs,bnskd->bnqkgd', p[..., :n_loc], vb)
         + jnp.einsum('bnkgqs,bskd->bnqkgd', p[..., n_loc:n_loc + C], vc))
    return o.reshape(B, n, H * dh)


def global_attention(q, k, v, kc, vc):
    B, n, H, dh = q.shape
    nb = n // Q_BLOCK
    k_all = jnp.concatenate([k, kc], axis=1)
    v_all = jnp.concatenate([v, vc], axis=1)
    qb = q.reshape(B, nb, Q_BLOCK, H, dh).transpose(1, 0, 2, 3, 4)
    o = lax.map(lambda qblk: dense_attention(qblk, k_all, v_all, None), qb)
    return o.transpose(1, 0, 2, 3).reshape(B, n, H * dh)


def setup_inputs(seed: int = 0) -> dict:
    key = jax.random.key(seed)
    ks = jax.random.split(key, 32)

    def nrm(k, shape, scale):
        return jax.random.normal(k, shape, F32) * scale

    D = D_MODEL
    a_c = jax.random.uniform(ks[27], (DEPTH, 2, D_LRU), F32, 0.9, 0.999) ** (1.0 / LRU_C)
    return {
        'x': nrm(ks[0], (BATCH, SEQ, D), 1.0),
        'c': nrm(ks[1], (BATCH, D), 1.0),
        'ctx': nrm(ks[2], (BATCH, CTX_LEN, D), 1.0),
        'c_ctx': nrm(ks[3], (D,), 1.0),
        'w_mod': nrm(ks[4], (DEPTH, D, N_MOD * D), 0.5 * D ** -0.5),
        'b_mod': nrm(ks[5], (DEPTH, N_MOD * D), 0.02),
        'norm_g': 1.0 + nrm(ks[6], (DEPTH, 3, D), 0.02),
        'ffn1_w13': nrm(ks[7], (DEPTH, D, 2 * D_FF), D ** -0.5),
        'ffn1_w2': nrm(ks[8], (DEPTH, D_FF, D), D_FF ** -0.5),
        'ffn2_w13': nrm(ks[9], (DEPTH, D, 2 * D_FF), D ** -0.5),
        'ffn2_w2': nrm(ks[10], (DEPTH, D_FF, D), D_FF ** -0.5),
        'w_in': nrm(ks[11], (DEPTH, D, D_IN), D ** -0.5),
        'w_out': nrm(ks[12], (DEPTH, D_MIX, D), D_MIX ** -0.5),
        'hy_conv_w': nrm(ks[13], (DEPTH, HY_SHORT, 3 * D_HY), HY_SHORT ** -0.5),
        'hy_conv_b': nrm(ks[14], (DEPTH, 3 * D_HY), 0.02),
        'hy_w1': nrm(ks[15], (DEPTH, HY_EMB, HY_HIDDEN), HY_EMB ** -0.5),
        'hy_b1': nrm(ks[16], (DEPTH, HY_HIDDEN), 0.02),
        'hy_freq': 1.0 + nrm(ks[17], (DEPTH, 2, HY_HIDDEN), 0.1),
        'hy_w2': nrm(ks[18], (DEPTH, HY_HIDDEN, HY_HIDDEN), HY_HIDDEN ** -0.5),
        'hy_b2': nrm(ks[19], (DEPTH, HY_HIDDEN), 0.02),
        'hy_w3': nrm(ks[20], (DEPTH, HY_HIDDEN, HY_ORDER * 2 * D_HY), HY_HIDDEN ** -0.5),
        'hy_skip': nrm(ks[21], (DEPTH, HY_ORDER, D_HY), 0.5),
        'lru_conv_w': nrm(ks[22], (DEPTH, LRU_SHORT, D_LRU), LRU_SHORT ** -0.5),
        'lru_conv_b': nrm(ks[23], (DEPTH, D_LRU), 0.02),
        'lru_wa': nrm(ks[24], (DEPTH, 2, LRU_BLOCKS, LRU_BLOCK_DIM, LRU_BLOCK_DIM), LRU_BLOCK_DIM ** -0.5),
        'lru_ba': nrm(ks[25], (DEPTH, 2, D_LRU), 0.02),
        'lru_wx': nrm(ks[26], (DEPTH, 2, LRU_BLOCKS, LRU_BLOCK_DIM, LRU_BLOCK_DIM), LRU_BLOCK_DIM ** -0.5),
        'lru_bx': nrm(ks[28], (DEPTH, 2, D_LRU), 0.02),
        'lru_lambda': jnp.log(a_c) - jnp.log1p(-a_c),
        'win_sink': nrm(ks[29], (DEPTH, N_HEADS_WIN), 1.0),
        'qk_gain': 1.0 + nrm(ks[30], (DEPTH, 2, HEAD_DIM), 0.02),
        'final_g': 1.0 + nrm(ks[31], (D,), 0.02),
    }


def reference(x, c, ctx, c_ctx, w_mod, b_mod, norm_g, ffn1_w13, ffn1_w2, ffn2_w13, ffn2_w2, w_in, w_out,
              hy_conv_w, hy_conv_b, hy_w1, hy_b1, hy_freq, hy_w2, hy_b2, hy_w3, hy_skip,
              lru_conv_w, lru_conv_b, lru_wa, lru_ba, lru_wx, lru_bx, lru_lambda,
              win_sink, qk_gain, final_g):
    n = x.shape[1]
    n_ctx = ctx.shape[1]
    cos, sin = axial_rope(n)
    for l in range(DEPTH):
        last = l == DEPTH - 1
        ml = jnp.split((jax.nn.silu(c) @ w_mod[l] + b_mod[l])[:, None, :], N_MOD, axis=-1)
        mc = jnp.split((jax.nn.silu(c_ctx) @ w_mod[l] + b_mod[l])[None, None, :], N_MOD, axis=-1)

        x = x + 0.5 * ml[2] * swiglu(adaln_in(x, norm_g[l, 0], ml[0], ml[1]), ffn1_w13[l], ffn1_w2[l])
        ctx = ctx + 0.5 * mc[2] * swiglu(adaln_in(ctx, norm_g[l, 0], mc[0], mc[1]), ffn1_w13[l], ffn1_w2[l])

        hl = adaln_in(x, norm_g[l, 1], ml[3], ml[4])
        hc = adaln_in(ctx, norm_g[l, 1], mc[3], mc[4])
        pl = jnp.split(hl @ w_in[l], SPLIT_IDX, axis=-1)
        pc = jnp.split(hc @ w_in[l], SPLIT_IDX, axis=-1)

        kf_l = hyena_filter_spectrum(n, hy_w1[l], hy_b1[l], hy_freq[l], hy_w2[l], hy_b2[l], hy_w3[l])
        y_hy = hyena_mixer(depthwise_conv(pl[0], hy_conv_w[l], hy_conv_b[l], HY_PAD), kf_l, hy_skip[l])

        r_l = depthwise_conv(pl[1], lru_conv_w[l], lru_conv_b[l], LRU_PAD)
        r_c = depthwise_conv(pc[1], lru_conv_w[l], lru_conv_b[l], LRU_PAD)
        h_l, h_c = rglru_bidirectional(r_l, r_c, lru_wa[l], lru_ba[l], lru_wx[l], lru_bx[l], lru_lambda[l])
        y_lru = jax.nn.gelu(pl[2]) * h_l.astype(x.dtype)

        kwc, vwc = heads(pc[4]), heads(pc[5])
        y_win = window_attention(apply_rope(heads(pl[3]), cos, sin), apply_rope(heads(pl[4]), cos, sin),
                                 heads(pl[5]), kwc, vwc, win_sink[l])

        kgc, vgc = rmsnorm(heads(pc[7]), qk_gain[l, 1]), heads(pc[8])
        y_glb = global_attention(apply_rope(rmsnorm(heads(pl[6]), qk_gain[l, 0]), cos, sin),
                                 apply_rope(rmsnorm(heads(pl[7]), qk_gain[l, 1]), cos, sin),
                                 heads(pl[8]), kgc, vgc)

        x = x + ml[5] * (jnp.concatenate([y_hy, y_lru, y_win, y_glb], axis=-1) @ w_out[l])

        if not last:
            kf_c = hyena_filter_spectrum(n_ctx, hy_w1[l], hy_b1[l], hy_freq[l], hy_w2[l], hy_b2[l], hy_w3[l])
            yc_hy = hyena_mixer(depthwise_conv(pc[0], hy_conv_w[l], hy_conv_b[l], HY_PAD), kf_c, hy_skip[l])
            yc_lru = jax.nn.gelu(pc[2]) * h_c.astype(ctx.dtype)
            yc_win = dense_attention(heads(pc[3]), kwc, vwc, win_sink[l])
            yc_glb = dense_attention(rmsnorm(heads(pc[6]), qk_gain[l, 0]), kgc, vgc, None)
            ctx = ctx + mc[5] * (jnp.concatenate([yc_hy, yc_lru, yc_win, yc_glb], axis=-1) @ w_out[l])

        x = x + 0.5 * ml[8] * swiglu(adaln_in(x, norm_g[l, 2], ml[6], ml[7]), ffn2_w13[l], ffn2_w2[l])
        if not last:
            ctx = ctx + 0.5 * mc[8] * swiglu(adaln_in(ctx, norm_g[l, 2], mc[6], mc[7]), ffn2_w13[l], ffn2_w2[l])

    return rmsnorm(x, final_g)
```

```python
import functools
import math

import numpy as np
import jax
import jax.numpy as jnp
from jax import lax
from jax.experimental import pallas as pl
from jax.experimental.pallas import tpu as pltpu

F32 = jnp.float32
BF16 = jnp.bfloat16

HEAD_DIM = 64
GRID_W = 64
N_MOD = 9
WINDOW = 128
Q_BLOCK = 128
HY_ORDER = 2
HY_BANDS = 16
HY_MIN_DECAY = math.log(1e-2) / 1.5
HY_MAX_DECAY = math.log(1e-2) / 0.3
LRU_BLOCKS = 4
LRU_C = 8.0
ROPE_BASE = 10000.0
EPS = 1e-6
NEG_INF = -1e30

LANES = 128
SUBLANES = 8
VMEM_LIMIT = 56 * 1024 * 1024

TOKEN_TILE = 512
FF_CHUNK = 256
CONV_TILE = 256
SCAN_TILE = 256
FLASH_TQ = 256
FLASH_TK = 512
FFT_F1_TILE = 8
MM_COL_TILE = 2048


def _params(sem, vmem=None):
    return pltpu.CompilerParams(dimension_semantics=sem, vmem_limit_bytes=vmem)


def _split_bf16(x):
    hi = x.astype(BF16)
    lo = (x - hi.astype(F32)).astype(BF16)
    return hi, lo


def _dot(a, b):
    return jnp.dot(a, b, preferred_element_type=F32)


def _dot_nt(a, b):
    return lax.dot_general(a, b, (((1,), (1,)), ((), ())), preferred_element_type=F32)


def _dot3_const(ch, cl, x):
    xh, xl = _split_bf16(x)
    return _dot(ch, xh) + (_dot(cl, xh) + _dot(ch, xl))


def _mod_kernel(c_ref, w_ref, b_ref, o_ref):
    c = c_ref[...]
    a = c * jax.nn.sigmoid(c)
    ah, al = _split_bf16(a)
    wh, wl = _split_bf16(w_ref[0])
    o_ref[0] = _dot(ah, wh) + (_dot(al, wh) + _dot(ah, wl)) + b_ref[0]


def _modulation(cond, w_mod, b_mod):
    depth, d, nd = w_mod.shape
    tn = 1152
    assert nd % tn == 0
    return pl.pallas_call(
        _mod_kernel,
        grid=(depth, nd // tn),
        in_specs=[
            pl.BlockSpec((SUBLANES, d), lambda l, j: (0, 0)),
            pl.BlockSpec((1, d, tn), lambda l, j: (l, 0, j)),
            pl.BlockSpec((1, 1, tn), lambda l, j: (l, 0, j)),
        ],
        out_specs=pl.BlockSpec((1, SUBLANES, tn), lambda l, j: (l, 0, j)),
        out_shape=jax.ShapeDtypeStruct((depth, SUBLANES, nd), F32),
        compiler_params=_params(("parallel", "parallel")),
        name="modulation",
    )(cond, w_mod, b_mod.reshape(depth, 1, nd))


def _adaln(x, g, shift, scale):
    ms = jnp.mean(x * x, axis=-1, keepdims=True)
    y = x * lax.rsqrt(ms + EPS) * g
    return y * (1.0 + scale) + shift


def _group_map(rows_per_group, tile):
    tiles = rows_per_group // tile
    return lambda i: (i // tiles, 0, 0)


def _ffn_kernel(x_ref, mod_ref, g_ref, w1_ref, w3_ref, w2_ref, o_ref, xn_ref, acc_ref, *, mod_base, n_chunks):
    x = x_ref[...]
    shift = mod_ref[0, mod_base:mod_base + 1, :]
    scale = mod_ref[0, mod_base + 1:mod_base + 2, :]
    gate = mod_ref[0, mod_base + 2:mod_base + 3, :]
    xn_ref[...] = _adaln(x, g_ref[...], shift, scale).astype(BF16)
    acc_ref[...] = jnp.zeros_like(acc_ref)

    def body(c, carry):
        xn = xn_ref[...]
        h = _dot(xn, w1_ref[c])
        u = _dot(xn, w3_ref[c])
        a = (h * jax.nn.sigmoid(h) * u).astype(BF16)
        acc_ref[...] += _dot(a, w2_ref[c])
        return carry

    lax.fori_loop(0, n_chunks, body, 0)
    o_ref[...] = x + 0.5 * gate * acc_ref[...]


def _ffn(x, mod, g, w1c, w3c, w2c, rows_per_group, mod_base):
    rows, d = x.shape
    tm = TOKEN_TILE
    assert rows % tm == 0 and rows_per_group % tm == 0
    n_chunks, _, ch = w1c.shape
    const3 = lambda i: (0, 0, 0)
    return pl.pallas_call(
        functools.partial(_ffn_kernel, mod_base=mod_base, n_chunks=n_chunks),
        grid=(rows // tm,),
        in_specs=[
            pl.BlockSpec((tm, d), lambda i: (i, 0)),
            pl.BlockSpec((1, N_MOD, d), _group_map(rows_per_group, tm)),
            pl.BlockSpec((1, d), lambda i: (0, 0)),
            pl.BlockSpec((n_chunks, d, ch), const3, pipeline_mode=pl.Buffered(1)),
            pl.BlockSpec((n_chunks, d, ch), const3, pipeline_mode=pl.Buffered(1)),
            pl.BlockSpec((n_chunks, ch, d), const3, pipeline_mode=pl.Buffered(1)),
        ],
        out_specs=pl.BlockSpec((tm, d), lambda i: (i, 0)),
        out_shape=jax.ShapeDtypeStruct((rows, d), F32),
        scratch_shapes=[pltpu.VMEM((tm, d), BF16), pltpu.VMEM((tm, d), F32)],
        compiler_params=_params(("parallel",), VMEM_LIMIT),
        name="ffn",
    )(x, mod, g, w1c, w3c, w2c)


def _ffn_weights(w13, w2):
    d, two_ff = w13.shape
    ff = two_ff // 2
    assert ff % FF_CHUNK == 0
    nc = ff // FF_CHUNK
    w1c = w13[:, :ff].astype(BF16).reshape(d, nc, FF_CHUNK).transpose(1, 0, 2)
    w3c = w13[:, ff:].astype(BF16).reshape(d, nc, FF_CHUNK).transpose(1, 0, 2)
    w2c = w2.astype(BF16).reshape(nc, FF_CHUNK, d)
    return w1c, w3c, w2c


def _swap_halves(z):
    lane = lax.broadcasted_iota(jnp.int32, z.shape, 1)
    first = (lane % HEAD_DIM) < (HEAD_DIM // 2)
    return jnp.where(first, pltpu.roll(z, LANES - HEAD_DIM // 2, 1), pltpu.roll(z, HEAD_DIM // 2, 1))


def _mixin_kernel(*refs, use_rope, d_hy3, d_lru, d_q, d_kv):
    if use_rope:
        (x_ref, mod_ref, g_ref, w_ref, gq_ref, gk_ref, hs_ref, cos_ref, sin_ref,
         uin_ref, gate_ref, qw_ref, kw_ref, vw_ref, qg_ref, kg_ref, vg_ref) = refs
    else:
        (x_ref, mod_ref, g_ref, w_ref, gq_ref, gk_ref, hs_ref,
         uin_ref, gate_ref, qw_ref, kw_ref, vw_ref, qg_ref, kg_ref, vg_ref) = refs
        cos_ref = sin_ref = None
    x = x_ref[...]
    xn = _adaln(x, g_ref[...], mod_ref[0, 3:4, :], mod_ref[0, 4:5, :]).astype(BF16)

    def proj(lo, width):
        return _dot(xn, w_ref[:, lo:lo + width])

    def rope(z):
        if not use_rope:
            return z
        return z * cos_ref[...] + _swap_halves(z) * sin_ref[...]

    def head_norm(z, gain):
        z2 = z * z
        zh, zl = _split_bf16(z2)
        ms = _dot(zh, hs_ref[...]) + _dot(zl, hs_ref[...])
        return z * lax.rsqrt(ms + EPS) * gain

    off = 0
    uin_ref[...] = proj(off, d_hy3 + d_lru)
    off += d_hy3 + d_lru
    gate_ref[...] = proj(off, d_lru)
    off += d_lru
    scale = HEAD_DIM ** -0.5
    for c in range(d_q // LANES):
        qw_ref[:, c * LANES:(c + 1) * LANES] = (rope(proj(off + c * LANES, LANES)) * scale).astype(BF16)
    off += d_q
    for c in range(d_kv // LANES):
        kw_ref[:, c * LANES:(c + 1) * LANES] = rope(proj(off + c * LANES, LANES)).astype(BF16)
    off += d_kv
    vw_ref[...] = proj(off, d_kv).astype(BF16)
    off += d_kv
    for c in range(d_q // LANES):
        z = head_norm(proj(off + c * LANES, LANES), gq_ref[...])
        qg_ref[:, c * LANES:(c + 1) * LANES] = (rope(z) * scale).astype(BF16)
    off += d_q
    for c in range(d_kv // LANES):
        z = head_norm(proj(off + c * LANES, LANES), gk_ref[...])
        kg_ref[:, c * LANES:(c + 1) * LANES] = rope(z).astype(BF16)
    off += d_kv
    vg_ref[...] = proj(off, d_kv).astype(BF16)


def _mixin(x, mod, g, w_in, gq, gk, hsum, rope_tabs, rows_per_group, seq_len, dims):
    rows, d = x.shape
    d_hy3, d_lru, d_q, d_kv = dims
    tm = TOKEN_TILE
    assert rows % tm == 0 and rows_per_group % tm == 0
    use_rope = rope_tabs is not None
    d_in = w_in.shape[1]
    row = lambda i: (i, 0)
    const2 = lambda i: (0, 0)
    in_specs = [
        pl.BlockSpec((tm, d), row),
        pl.BlockSpec((1, N_MOD, d), _group_map(rows_per_group, tm)),
        pl.BlockSpec((1, d), const2),
        pl.BlockSpec((d, d_in), const2, pipeline_mode=pl.Buffered(1)),
        pl.BlockSpec((1, LANES), const2),
        pl.BlockSpec((1, LANES), const2),
        pl.BlockSpec((LANES, LANES), const2),
    ]
    args = [x, mod, g, w_in, gq, gk, hsum]
    if use_rope:
        assert seq_len % tm == 0
        tiles_per_seq = seq_len // tm
        tab = lambda i: (i % tiles_per_seq, 0)
        in_specs += [pl.BlockSpec((tm, LANES), tab), pl.BlockSpec((tm, LANES), tab)]
        args += list(rope_tabs)
    widths = (d_hy3 + d_lru, d_lru, d_q, d_kv, d_kv, d_q, d_kv, d_kv)
    dtypes = (F32, F32, BF16, BF16, BF16, BF16, BF16, BF16)
    return pl.pallas_call(
        functools.partial(_mixin_kernel, use_rope=use_rope, d_hy3=d_hy3, d_lru=d_lru, d_q=d_q, d_kv=d_kv),
        grid=(rows // tm,),
        in_specs=in_specs,
        out_specs=[pl.BlockSpec((tm, w), row) for w in widths],
        out_shape=[jax.ShapeDtypeStruct((rows, w), dt) for w, dt in zip(widths, dtypes)],
        compiler_params=_params(("parallel",), VMEM_LIMIT),
        name="mixin_rope" if use_rope else "mixin",
    )(*args)


def _shortconv_kernel(cur_ref, prev_ref, next_ref, w_ref, b_ref, o_ref, ext_ref, *, tiles_per_seq):
    ts = cur_ref.shape[0]
    j = pl.program_id(0) % tiles_per_seq
    keep_prev = (j > 0).astype(F32)
    keep_next = (j < tiles_per_seq - 1).astype(F32)
    ext_ref[0:SUBLANES, :] = prev_ref[...] * keep_prev
    ext_ref[SUBLANES:SUBLANES + ts, :] = cur_ref[...]
    ext_ref[SUBLANES + ts:, :] = next_ref[...] * keep_next
    acc = b_ref[...] + w_ref[2:3, :] * cur_ref[...]
    for k in (0, 1, 3):
        acc = acc + w_ref[k:k + 1, :] * ext_ref[SUBLANES - 2 + k:SUBLANES - 2 + k + ts, :]
    o_ref[...] = acc


def _shortconv(u, w4, bias, seq_len):
    rows, ch = u.shape
    ts = min(CONV_TILE, seq_len)
    assert seq_len % ts == 0 and rows % ts == 0 and ts % SUBLANES == 0
    tiles_per_seq = seq_len // ts
    sub = ts // SUBLANES
    last = rows // SUBLANES - 1
    return pl.pallas_call(
        functools.partial(_shortconv_kernel, tiles_per_seq=tiles_per_seq),
        grid=(rows // ts,),
        in_specs=[
            pl.BlockSpec((ts, ch), lambda i: (i, 0)),
            pl.BlockSpec((SUBLANES, ch), lambda i: (jnp.maximum(i * sub - 1, 0), 0)),
            pl.BlockSpec((SUBLANES, ch), lambda i: (jnp.minimum((i + 1) * sub, last), 0)),
            pl.BlockSpec((4, ch), lambda i: (0, 0)),
            pl.BlockSpec((1, ch), lambda i: (0, 0)),
        ],
        out_specs=pl.BlockSpec((ts, ch), lambda i: (i, 0)),
        out_shape=jax.ShapeDtypeStruct((rows, ch), F32),
        scratch_shapes=[pltpu.VMEM((ts + 2 * SUBLANES, ch), F32)],
        compiler_params=_params(("parallel",)),
        name="shortconv",
    )(u, u, u, w4, bias)


def _fft_split(n):
    big = 2 * n
    n2 = 128 if big >= 128 * 128 else 16
    assert big % n2 == 0
    return big // n2, n2


def _np_hi_lo(m):
    m32 = jnp.asarray(np.asarray(m, dtype=np.float32))
    hi = m32.astype(BF16)
    lo = (m32 - hi.astype(F32)).astype(BF16)
    return hi, lo


@functools.lru_cache(maxsize=None)
def _fft_tables(n):
    n1, n2 = _fft_split(n)
    big = n1 * n2
    h1 = n1 // 2
    f1 = np.arange(n1)[:, None].astype(np.float64)
    s1 = np.arange(h1)[None, :].astype(np.float64)
    th = 2.0 * np.pi * f1 * s1 / n1
    c, s = np.cos(th), np.sin(th)
    w1 = np.block([[c, s], [-s, c]])
    w3 = np.block([[c.T, -s.T], [s.T, c.T]]) / big
    s1f = np.arange(n1)[None, :].astype(np.float64)
    thf = 2.0 * np.pi * f1 * s1f / n1
    w1f = np.concatenate([np.cos(thf), -np.sin(thf)], axis=0)
    f2 = np.arange(n2)[:, None].astype(np.float64)
    s2 = np.arange(n2)[None, :].astype(np.float64)
    th2 = 2.0 * np.pi * f2 * s2 / n2
    c2, sn2 = np.cos(th2), np.sin(th2)
    m2 = np.block([[c2, sn2], [-sn2, c2]])
    ph = 2.0 * np.pi * np.arange(n1)[:, None] * np.arange(n2)[None, :] / big
    tw = np.stack([np.cos(ph), -np.sin(ph)], axis=0)
    tw = np.broadcast_to(tw[..., None], (2, n1, n2, LANES))
    return dict(n1=n1, n2=n2, w1=w1, w3=w3, w1f=w1f, m2=m2, m2t=m2.T, tw=np.ascontiguousarray(tw, dtype=np.float32))


def _mm3_kernel(wh_ref, wl_ref, x_ref, o_ref):
    o_ref[...] = _dot3_const(wh_ref[...], wl_ref[...], x_ref[...])


def _mm3(w, x):
    wh, wl = _np_hi_lo(w)
    m, k = wh.shape
    cols = x.shape[1]
    tn = min(MM_COL_TILE, cols)
    assert cols % tn == 0 and x.shape[0] == k
    return pl.pallas_call(
        _mm3_kernel,
        grid=(cols // tn,),
        in_specs=[
            pl.BlockSpec((m, k), lambda j: (0, 0)),
            pl.BlockSpec((m, k), lambda j: (0, 0)),
            pl.BlockSpec((k, tn), lambda j: (0, j)),
        ],
        out_specs=pl.BlockSpec((m, tn), lambda j: (0, j)),
        out_shape=jax.ShapeDtypeStruct((m, cols), F32),
        compiler_params=_params(("parallel",), VMEM_LIMIT),
        name="fft_outer",
    )(wh, wl, x)


def _fft_mid_kernel(*refs, fwd_only, tf, n2, ch):
    if fwd_only:
        a_ref, tw_ref, mh_ref, ml_ref, o_ref = refs
    else:
        a_ref, k_ref, tw_ref, mh_ref, ml_ref, mth_ref, mtl_ref, o_ref = refs
    reps = ch // LANES
    for t in range(tf):
        twr = tw_ref[0, t]
        twi = tw_ref[1, t]
        if reps > 1:
            twr = jnp.concatenate([twr] * reps, axis=1)
            twi = jnp.concatenate([twi] * reps, axis=1)
        ar = a_ref[0, t]
        ai = a_ref[1, t]
        b = jnp.concatenate([ar * twr - ai * twi, ar * twi + ai * twr], axis=0)
        x = _dot3_const(mh_ref[...], ml_ref[...], b)
        xr, xi = x[:n2], x[n2:]
        if fwd_only:
            o_ref[0, t] = xr
            o_ref[1, t] = xi
        else:
            kr = k_ref[0, t]
            ki = k_ref[1, t]
            y = jnp.concatenate([xr * kr - xi * ki, xr * ki + xi * kr], axis=0)
            c = _dot3_const(mth_ref[...], mtl_ref[...], y)
            cr, ci = c[:n2], c[n2:]
            o_ref[0, t] = cr * twr + ci * twi
            o_ref[1, t] = ci * twr - cr * twi


def _fft_mid(a4, spec4, order, tabs, fwd_only):
    _, n1, n2, ch = a4.shape
    tf = FFT_F1_TILE
    assert n1 % tf == 0 and ch % LANES == 0
    blk = lambda i: (0, i, 0, 0)
    const2 = lambda i: (0, 0)
    mh, ml = _np_hi_lo(tabs["m2"])
    tw = jnp.asarray(tabs["tw"])
    in_specs = [pl.BlockSpec((2, tf, n2, ch), blk)]
    args = [a4]
    if not fwd_only:
        in_specs.append(pl.BlockSpec((2, tf, n2, ch), lambda i: (0, i, 0, order)))
        args.append(spec4)
    in_specs += [pl.BlockSpec((2, tf, n2, LANES), blk),
                 pl.BlockSpec((2 * n2, 2 * n2), const2), pl.BlockSpec((2 * n2, 2 * n2), const2)]
    args += [tw, mh, ml]
    if not fwd_only:
        mth, mtl = _np_hi_lo(tabs["m2t"])
        in_specs += [pl.BlockSpec((2 * n2, 2 * n2), const2), pl.BlockSpec((2 * n2, 2 * n2), const2)]
        args += [mth, mtl]
    return pl.pallas_call(
        functools.partial(_fft_mid_kernel, fwd_only=fwd_only, tf=tf, n2=n2, ch=ch),
        grid=(n1 // tf,),
        in_specs=in_specs,
        out_specs=pl.BlockSpec((2, tf, n2, ch), blk),
        out_shape=jax.ShapeDtypeStruct((2, n1, n2, ch), F32),
        compiler_params=_params(("parallel",), VMEM_LIMIT),
        name="fft_mid_fwd" if fwd_only else "fft_mid",
    )(*args)


def _filter_spectrum(k, tabs):
    n1, n2 = tabs["n1"], tabs["n2"]
    cp = k.shape[1]
    a = _mm3(tabs["w1f"], k.reshape(n1, n2 * cp))
    return _fft_mid(a.reshape(2, n1, n2, cp), None, 0, tabs, True)


def _long_conv(v, spec4, order, tabs):
    n1, n2 = tabs["n1"], tabs["n2"]
    ch = v.shape[1]
    a = _mm3(tabs["w1"], v.reshape(n1, n2 * ch))
    d = _fft_mid(a.reshape(2, n1, n2, ch), spec4, order, tabs, False)
    y = _mm3(tabs["w3"], d.reshape(2 * n1, n2 * ch))
    return y.reshape(v.shape)


def _hyena_filter(n, w1, b1, freq, w2, b2, w3, d_hy):
    hp = lax.Precision.HIGHEST
    t = jnp.linspace(0.0, 1.0, n, dtype=F32)[:, None]
    w = 2.0 * math.pi * jnp.arange(n, dtype=F32)[:, None] / n
    f = jnp.linspace(1e-4, HY_BANDS - 1, HY_BANDS, dtype=F32)[None, :]
    z = jnp.concatenate([t, jnp.cos(f * w), -jnp.sin(f * w)], axis=-1)
    h = jnp.sin(freq[0] * (jnp.dot(z, w1, precision=hp) + b1))
    h = jnp.sin(freq[1] * (jnp.dot(h, w2, precision=hp) + b2))
    h = jnp.dot(h, w3, precision=hp).astype(F32).reshape(n, HY_ORDER, 2, d_hy)
    deltas = jnp.abs(jnp.linspace(HY_MIN_DECAY, HY_MAX_DECAY, d_hy, dtype=F32))
    h = h * jnp.exp(-t[:, :, None, None] * deltas)
    h = h / (jnp.sum(jnp.abs(h), axis=(0, 2), keepdims=True) + EPS)
    fwd, bwd = h[:, :, 0], h[:, :, 1]
    k = jnp.concatenate([fwd, jnp.zeros_like(fwd[:1]), bwd[:0:-1]], axis=0)
    return k.reshape(2 * n, HY_ORDER * d_hy)


def _gate_kernel(a_ref, b_ref, c_ref, s_ref, o_ref):
    o_ref[...] = a_ref[...] * (b_ref[...] + c_ref[...] * s_ref[...])


def _hyena_gate(uc, a_col, conv, c_src, c_col, skip):
    rows, ch = conv.shape
    tm = min(TOKEN_TILE, rows)
    assert rows % tm == 0
    return pl.pallas_call(
        _gate_kernel,
        grid=(rows // tm,),
        in_specs=[
            pl.BlockSpec((tm, ch), lambda i: (i, a_col)),
            pl.BlockSpec((tm, ch), lambda i: (i, 0)),
            pl.BlockSpec((tm, ch), lambda i: (i, c_col)),
            pl.BlockSpec((1, ch), lambda i: (0, 0)),
        ],
        out_specs=pl.BlockSpec((tm, ch), lambda i: (i, 0)),
        out_shape=jax.ShapeDtypeStruct((rows, ch), F32),
        compiler_params=_params(("parallel",)),
        name="hyena_gate",
    )(uc, conv, c_src, skip)


def _shift_rows(x, s, fill, reverse):
    t = x.shape[0]
    row = lax.broadcasted_iota(jnp.int32, x.shape, 0)
    if reverse:
        return jnp.where(row >= t - s, fill, pltpu.roll(x, t - s, 0))
    return jnp.where(row < s, fill, pltpu.roll(x, s, 0))


def _lru_kernel(uf_ref, ub_ref, wa_ref, wx_ref, p_ref, h0_ref, hf_ref, hb_ref, hl_ref, carry_ref):
    j = pl.program_id(1)
    nt = pl.num_programs(1)
    tl = uf_ref.shape[0]

    @pl.when(j == 0)
    def _():
        carry_ref[...] = h0_ref[0]

    for d, (u_ref, o_ref) in enumerate(((uf_ref, hf_ref), (ub_ref, hb_ref))):
        reverse = d == 1
        u = u_ref[...]
        ub16 = u.astype(BF16)
        r = jax.nn.sigmoid(_dot(ub16, wa_ref[d]) + p_ref[d, 0:1, :])
        gi = jax.nn.sigmoid(_dot(ub16, wx_ref[d]) + p_ref[d, 1:2, :])
        nlam = -p_ref[d, 2:3, :]
        softplus = jnp.maximum(nlam, 0.0) + jnp.log1p(jnp.exp(-jnp.abs(nlam)))
        log_a = -LRU_C * r * softplus
        a = jnp.exp(log_a)
        b = jnp.sqrt(-jnp.tanh(log_a) * (a * a + 1.0)) * gi * u
        s = 1
        while s < tl:
            a_s = _shift_rows(a, s, 1.0, reverse)
            b_s = _shift_rows(b, s, 0.0, reverse)
            b = a * b_s + b
            a = a * a_s
            s *= 2
        h = b + a * carry_ref[d:d + 1, :]
        o_ref[...] = h
        carry_ref[d:d + 1, :] = h[0:1, :] if reverse else h[tl - 1:tl, :]

    @pl.when(j == nt - 1)
    def _():
        hl_ref[0] = carry_ref[...]


def _lru(uc, col, wa, wx, p, h0, batch, seq_len):
    ch = wa.shape[-1]
    tl = min(SCAN_TILE, seq_len)
    assert seq_len % tl == 0
    nt = seq_len // tl
    rows = batch * seq_len
    fwd = lambda b, j: (b * nt + j, col)
    bwd = lambda b, j: (b * nt + nt - 1 - j, col)
    fwd_o = lambda b, j: (b * nt + j, 0)
    bwd_o = lambda b, j: (b * nt + nt - 1 - j, 0)
    const3 = lambda b, j: (0, 0, 0)
    return pl.pallas_call(
        _lru_kernel,
        grid=(batch, nt),
        in_specs=[
            pl.BlockSpec((tl, ch), fwd),
            pl.BlockSpec((tl, ch), bwd),
            pl.BlockSpec((2, ch, ch), const3),
            pl.BlockSpec((2, ch, ch), const3),
            pl.BlockSpec((2, 3, ch), const3),
            pl.BlockSpec((1, 2, ch), lambda b, j: (b, 0, 0)),
        ],
        out_specs=[
            pl.BlockSpec((tl, ch), fwd_o),
            pl.BlockSpec((tl, ch), bwd_o),
            pl.BlockSpec((1, 2, ch), lambda b, j: (b, 0, 0)),
        ],
        out_shape=[
            jax.ShapeDtypeStruct((rows, ch), F32),
            jax.ShapeDtypeStruct((rows, ch), F32),
            jax.ShapeDtypeStruct((batch, 2, ch), F32),
        ],
        scratch_shapes=[pltpu.VMEM((2, ch), F32)],
        compiler_params=_params(("parallel", "arbitrary")),
        name="rglru",
    )(uc, uc, wa, wx, p, h0)


def _block_diag(w):
    two, nb, d, _ = w.shape
    eye = jnp.eye(nb, dtype=w.dtype)
    return jnp.einsum('xnde,nm->xndme', w, eye).reshape(two, nb * d, nb * d)


def _softmax_parts(parts, sink):
    m = sink
    for s in parts:
        m = jnp.maximum(m, jnp.max(s, axis=-1, keepdims=True))
    den = jnp.exp(sink - m)
    ps = []
    for s in parts:
        p = jnp.exp(s - m)
        den = den + jnp.sum(p, axis=-1, keepdims=True)
        ps.append(p.astype(BF16))
    return ps, den


def _win_kernel(q_ref, kp_ref, kc_ref, kn_ref, vp_ref, vc_ref, vn_ref, kx_ref, vx_ref, sink_ref, o_ref, *, seq_len):
    i = pl.program_id(1)
    qb = q_ref.shape[0]
    dh = HEAD_DIM
    n_kv = kc_ref.shape[1] // dh
    group = q_ref.shape[1] // dh // n_kv
    row = lax.broadcasted_iota(jnp.int32, (qb, 3 * qb), 0)
    col = lax.broadcasted_iota(jnp.int32, (qb, 3 * qb), 1)
    kpos = (i - 1) * qb + col
    valid = (jnp.abs(col - qb - row) <= WINDOW) & (kpos >= 0) & (kpos < seq_len)
    outs = []
    for kv in range(n_kv):
        sl = slice(kv * dh, (kv + 1) * dh)
        kk = jnp.concatenate([kp_ref[:, sl], kc_ref[:, sl], kn_ref[:, sl]], axis=0)
        vv = jnp.concatenate([vp_ref[:, sl], vc_ref[:, sl], vn_ref[:, sl]], axis=0)
        kx = kx_ref[:, sl]
        vx = vx_ref[:, sl]
        for g in range(group):
            h = kv * group + g
            qh = q_ref[:, h * dh:(h + 1) * dh]
            s_loc = jnp.where(valid, _dot_nt(qh, kk), NEG_INF)
            s_ctx = _dot_nt(qh, kx)
            (p_loc, p_ctx), den = _softmax_parts((s_loc, s_ctx), sink_ref[h:h + 1, 0:1])
            o = _dot(p_loc, vv) + _dot(p_ctx, vx)
            outs.append(o / den)
    o_ref[...] = jnp.concatenate(outs, axis=1).astype(o_ref.dtype)


def _window_attention(q, k, v, kx, vx, sink_rows, batch, seq_len, ctx_len):
    qb = Q_BLOCK
    nb = seq_len // qb
    dq, dk = q.shape[1], k.shape[1]
    cur = lambda b, i: (b * nb + i, 0)
    prv = lambda b, i: (b * nb + jnp.maximum(i - 1, 0), 0)
    nxt = lambda b, i: (b * nb + jnp.minimum(i + 1, nb - 1), 0)
    cx = lambda b, i: (b, 0)
    return pl.pallas_call(
        functools.partial(_win_kernel, seq_len=seq_len),
        grid=(batch, nb),
        in_specs=[
            pl.BlockSpec((qb, dq), cur),
            pl.BlockSpec((qb, dk), prv), pl.BlockSpec((qb, dk), cur), pl.BlockSpec((qb, dk), nxt),
            pl.BlockSpec((qb, dk), prv), pl.BlockSpec((qb, dk), cur), pl.BlockSpec((qb, dk), nxt),
            pl.BlockSpec((ctx_len, dk), cx), pl.BlockSpec((ctx_len, dk), cx),
            pl.BlockSpec(sink_rows.shape, lambda b, i: (0, 0)),
        ],
        out_specs=pl.BlockSpec((qb, dq), cur),
        out_shape=jax.ShapeDtypeStruct((batch * seq_len, dq), BF16),
        compiler_params=_params(("parallel", "parallel")),
        name="window_attention",
    )(q, k, k, k, v, v, v, kx, vx, sink_rows)


def _dense_kernel(q_ref, k_ref, v_ref, sink_ref, o_ref):
    dh = HEAD_DIM
    n_kv = k_ref.shape[1] // dh
    group = q_ref.shape[1] // dh // n_kv
    outs = []
    for kv in range(n_kv):
        sl = slice(kv * dh, (kv + 1) * dh)
        kk = k_ref[:, sl]
        vv = v_ref[:, sl]
        for g in range(group):
            h = kv * group + g
            s = _dot_nt(q_ref[:, h * dh:(h + 1) * dh], kk)
            (p,), den = _softmax_parts((s,), sink_ref[h:h + 1, 0:1])
            outs.append(_dot(p, vv) / den)
    o_ref[...] = jnp.concatenate(outs, axis=1).astype(o_ref.dtype)


def _dense_attention(q, k, v, sink_rows, batch, seq_len):
    dq, dk = q.shape[1], k.shape[1]
    blk = lambda b: (b, 0)
    return pl.pallas_call(
        _dense_kernel,
        grid=(batch,),
        in_specs=[pl.BlockSpec((seq_len, dq), blk), pl.BlockSpec((seq_len, dk), blk), pl.BlockSpec((seq_len, dk), blk),
                  pl.BlockSpec(sink_rows.shape, lambda b: (0, 0))],
        out_specs=pl.BlockSpec((seq_len, dq), blk),
        out_shape=jax.ShapeDtypeStruct((batch * seq_len, dq), BF16),
        compiler_params=_params(("parallel",)),
        name="dense_attention",
    )(q, k, v, sink_rows)


def _flash_kernel(q_ref, k_ref, v_ref, kx_ref, vx_ref, o_ref, qs_ref, m_ref, l_ref, acc_ref):
    ki = pl.program_id(2)
    nk = pl.num_programs(2)
    tq = q_ref.shape[0]
    dh = HEAD_DIM
    n_kv = k_ref.shape[1] // dh
    group = q_ref.shape[1] // dh // n_kv

    @pl.when(ki == 0)
    def _():
        for kv in range(n_kv):
            for g in range(group):
                h = kv * group + g
                qs_ref[kv, g * tq:(g + 1) * tq, :] = q_ref[:, h * dh:(h + 1) * dh]
        m_ref[...] = jnp.full_like(m_ref, NEG_INF)
        l_ref[...] = jnp.zeros_like(l_ref)
        acc_ref[...] = jnp.zeros_like(acc_ref)

    def step(kblk_ref, vblk_ref):
        for kv in range(n_kv):
            sl = slice(kv * dh, (kv + 1) * dh)
            s = _dot_nt(qs_ref[kv], kblk_ref[:, sl])
            m_old = m_ref[kv]
            m_new = jnp.maximum(m_old, jnp.max(s, axis=-1, keepdims=True))
            alpha = jnp.exp(m_old - m_new)
            p = jnp.exp(s - m_new)
            l_ref[kv] = alpha * l_ref[kv] + jnp.sum(p, axis=-1, keepdims=True)
            acc_ref[kv] = alpha * acc_ref[kv] + _dot(p.astype(BF16), vblk_ref[:, sl])
            m_ref[kv] = m_new

    @pl.when(ki == 0)
    def _():
        step(kx_ref, vx_ref)

    step(k_ref, v_ref)

    @pl.when(ki == nk - 1)
    def _():
        outs = []
        for kv in range(n_kv):
            o = acc_ref[kv] / l_ref[kv]
            for g in range(group):
                outs.append(o[g * tq:(g + 1) * tq, :])
        o_ref[...] = jnp.concatenate(outs, axis=1).astype(o_ref.dtype)


def _global_attention(q, k, v, kx, vx, batch, seq_len, ctx_len):
    tq = min(FLASH_TQ, seq_len)
    tk = min(FLASH_TK, seq_len)
    assert seq_len % tq == 0 and seq_len % tk == 0
    nq, nk = seq_len // tq, seq_len // tk
    dq, dk = q.shape[1], k.shape[1]
    n_kv = dk // HEAD_DIM
    group = dq // dk
    qmap = lambda b, i, j: (b * nq + i, 0)
    kmap = lambda b, i, j: (b * nk + j, 0)
    cx = lambda b, i, j: (b, 0)
    return pl.pallas_call(
        _flash_kernel,
        grid=(batch, nq, nk),
        in_specs=[
            pl.BlockSpec((tq, dq), qmap),
            pl.BlockSpec((tk, dk), kmap), pl.BlockSpec((tk, dk), kmap),
            pl.BlockSpec((ctx_len, dk), cx), pl.BlockSpec((ctx_len, dk), cx),
        ],
        out_specs=pl.BlockSpec((tq, dq), qmap),
        out_shape=jax.ShapeDtypeStruct((batch * seq_len, dq), BF16),
        scratch_shapes=[
            pltpu.VMEM((n_kv, group * tq, HEAD_DIM), BF16),
            pltpu.VMEM((n_kv, group * tq, 1), F32),
            pltpu.VMEM((n_kv, group * tq, 1), F32),
            pltpu.VMEM((n_kv, group * tq, HEAD_DIM), F32),
        ],
        compiler_params=_params(("parallel", "parallel", "arbitrary")),
        name="global_attention",
    )(q, k, v, kx, vx)


def _outproj_kernel(x_ref, mod_ref, x2_ref, conv_ref, z_ref, skip_ref, gate_ref, hf_ref, hb_ref,
                    yw_ref, yg_ref, w_ref, o_ref):
    y_hy = x2_ref[...] * (conv_ref[...] + z_ref[...] * skip_ref[...])
    y_lru = jax.nn.gelu(gate_ref[...]) * (hf_ref[...] + hb_ref[...])
    acc = _dot(y_hy.astype(BF16), w_ref[0])
    acc = acc + _dot(y_lru.astype(BF16), w_ref[1])
    acc = acc + _dot(yw_ref[...], w_ref[2])
    acc = acc + _dot(yg_ref[...], w_ref[3])
    o_ref[...] = x_ref[...] + mod_ref[0, 5:6, :] * acc


def _outproj(x, mod, uc, conv2, z, skip2, gate, hf, hb, yw, yg, w_out4, rows_per_group):
    rows, d = x.shape
    tm = TOKEN_TILE
    assert rows % tm == 0 and rows_per_group % tm == 0
    ch = conv2.shape[1]
    row = lambda i: (i, 0)
    return pl.pallas_call(
        _outproj_kernel,
        grid=(rows // tm,),
        in_specs=[
            pl.BlockSpec((tm, d), row),
            pl.BlockSpec((1, N_MOD, d), _group_map(rows_per_group, tm)),
            pl.BlockSpec((tm, ch), lambda i: (i, 2)),
            pl.BlockSpec((tm, ch), row),
            pl.BlockSpec((tm, ch), row),
            pl.BlockSpec((1, ch), lambda i: (0, 0)),
            pl.BlockSpec((tm, ch), row),
            pl.BlockSpec((tm, ch), row),
            pl.BlockSpec((tm, ch), row),
            pl.BlockSpec((tm, ch), row),
            pl.BlockSpec((tm, ch), row),
            pl.BlockSpec(w_out4.shape, lambda i: (0, 0, 0)),
        ],
        out_specs=pl.BlockSpec((tm, d), row),
        out_shape=jax.ShapeDtypeStruct((rows, d), F32),
        compiler_params=_params(("parallel",), VMEM_LIMIT),
        name="outproj",
    )(x, mod, uc, conv2, z, skip2, gate, hf, hb, yw, yg, w_out4)


def _rmsnorm_kernel(x_ref, g_ref, o_ref):
    x = x_ref[...]
    ms = jnp.mean(x * x, axis=-1, keepdims=True)
    o_ref[...] = x * lax.rsqrt(ms + EPS) * g_ref[...]


def _rmsnorm(x, g):
    rows, d = x.shape
    tm = TOKEN_TILE
    return pl.pallas_call(
        _rmsnorm_kernel,
        grid=(rows // tm,),
        in_specs=[pl.BlockSpec((tm, d), lambda i: (i, 0)), pl.BlockSpec((1, d), lambda i: (0, 0))],
        out_specs=pl.BlockSpec((tm, d), lambda i: (i, 0)),
        out_shape=jax.ShapeDtypeStruct((rows, d), F32),
        compiler_params=_params(("parallel",)),
        name="final_norm",
    )(x, g)


def _rope_tables(n):
    rows = n // GRID_W
    n_freq = HEAD_DIM // 4
    row = jnp.repeat(jnp.arange(rows, dtype=F32), GRID_W, total_repeat_length=n)
    col = jnp.tile(jnp.arange(GRID_W, dtype=F32), rows)
    inv = ROPE_BASE ** (-jnp.arange(n_freq, dtype=F32) / n_freq)
    ang = jnp.concatenate([row[:, None] * inv, col[:, None] * inv], axis=-1)
    cos, sin = jnp.cos(ang), jnp.sin(ang)
    reps = LANES // HEAD_DIM
    cos_t = jnp.tile(jnp.concatenate([cos, cos], axis=-1), (1, reps))
    sin_t = jnp.tile(jnp.concatenate([-sin, sin], axis=-1), (1, reps))
    return cos_t, sin_t


def _lane_rows(vals):
    h = vals.shape[0]
    out = jnp.zeros((SUBLANES, LANES), F32)
    return out.at[:h].set(jnp.broadcast_to(vals.astype(F32)[:, None], (h, LANES)))


def kernel(x, c, ctx, c_ctx, w_mod, b_mod, norm_g, ffn1_w13, ffn1_w2, ffn2_w13, ffn2_w2, w_in, w_out,
           hy_conv_w, hy_conv_b, hy_w1, hy_b1, hy_freq, hy_w2, hy_b2, hy_w3, hy_skip,
           lru_conv_w, lru_conv_b, lru_wa, lru_ba, lru_wx, lru_bx, lru_lambda,
           win_sink, qk_gain, final_g):
    batch, n, d = x.shape
    nc = ctx.shape[1]
    depth = w_mod.shape[0]
    assert batch == 2, "the two batch entries are packed as one complex signal in the long convolution"
    d_hy = hy_skip.shape[-1]
    d_lru = lru_conv_w.shape[-1]
    n_heads_win = win_sink.shape[-1]
    d_q = n_heads_win * HEAD_DIM
    d_kv = d_q // 2
    dims = (3 * d_hy, d_lru, d_q, d_kv)
    assert d_hy == d_lru == d_q and d_hy % LANES == 0

    xl = x.reshape(batch * n, d)
    xc = ctx.reshape(batch * nc, d)

    cond = jnp.zeros((SUBLANES, d), F32).at[:batch].set(c).at[batch].set(c_ctx)
    mods = _modulation(cond, w_mod, b_mod).reshape(depth, SUBLANES, N_MOD, d)

    cos_t, sin_t = _rope_tables(n)
    hsum = jnp.asarray(np.kron(np.eye(LANES // HEAD_DIM), np.full((HEAD_DIM, HEAD_DIM), 1.0 / HEAD_DIM)), dtype=BF16)
    tabs_l = _fft_tables(n)
    tabs_c = _fft_tables(nc)
    no_sink = jnp.full((SUBLANES, LANES), NEG_INF, F32)

    for l in range(depth):
        last = l == depth - 1
        mod_l = mods[l, :batch]
        mod_c = mods[l, batch:batch + 1]
        f1 = _ffn_weights(ffn1_w13[l], ffn1_w2[l])
        f2 = _ffn_weights(ffn2_w13[l], ffn2_w2[l])
        g0, g1, g2 = (norm_g[l, i].reshape(1, d) for i in range(3))

        xl = _ffn(xl, mod_l, g0, *f1, rows_per_group=n, mod_base=0)
        xc = _ffn(xc, mod_c, g0, *f1, rows_per_group=batch * nc, mod_base=0)

        w_in_l = w_in[l].astype(BF16)
        gq = jnp.tile(qk_gain[l, 0], LANES // HEAD_DIM).reshape(1, LANES)
        gk = jnp.tile(qk_gain[l, 1], LANES // HEAD_DIM).reshape(1, LANES)
        pl_ = _mixin(xl, mod_l, g1, w_in_l, gq, gk, hsum, (cos_t, sin_t), n, n, dims)
        pc_ = _mixin(xc, mod_c, g1, w_in_l, gq, gk, hsum, None, batch * nc, nc, dims)
        uin_l, gate_l, qw_l, kw_l, vw_l, qg_l, kg_l, vg_l = pl_
        uin_c, gate_c, qw_c, kw_c, vw_c, qg_c, kg_c, vg_c = pc_

        w4 = jnp.concatenate(
            [jnp.concatenate([jnp.zeros((1, 3 * d_hy), F32), hy_conv_w[l]], axis=0), lru_conv_w[l]], axis=1)
        b4 = jnp.concatenate([hy_conv_b[l], lru_conv_b[l]]).reshape(1, -1)
        uc_l = _shortconv(uin_l, w4, b4, n)
        uc_c = _shortconv(uin_c, w4, b4, nc)

        skip1 = hy_skip[l, 0].reshape(1, d_hy)
        skip2 = hy_skip[l, 1].reshape(1, d_hy)
        spec_l = _filter_spectrum(
            _hyena_filter(n, hy_w1[l], hy_b1[l], hy_freq[l], hy_w2[l], hy_b2[l], hy_w3[l], d_hy), tabs_l)
        c1 = _long_conv(uc_l[:, :d_hy], spec_l, 0, tabs_l)
        z_l = _hyena_gate(uc_l, 1, c1, uc_l, 0, skip1)
        c2_l = _long_conv(z_l, spec_l, 1, tabs_l)

        wa = _block_diag(lru_wa[l]).astype(BF16)
        wx = _block_diag(lru_wx[l]).astype(BF16)
        lp = jnp.stack([lru_ba[l], lru_bx[l], lru_lambda[l]], axis=1)
        lru_col = 3 * d_hy // d_lru
        hf_c, hb_c, h_last = _lru(uc_c, lru_col, wa, wx, lp, jnp.zeros((batch, 2, d_lru), F32), batch, nc)
        hf_l, hb_l, _ = _lru(uc_l, lru_col, wa, wx, lp, h_last, batch, n)

        sink_rows = _lane_rows(win_sink[l])
        yw_l = _window_attention(qw_l, kw_l, vw_l, kw_c, vw_c, sink_rows, batch, n, nc)

        yg_l = _global_attention(qg_l, kg_l, vg_l, kg_c, vg_c, batch, n, nc)

        w_out4 = w_out[l].astype(BF16).reshape(4, d_hy, d)
        xl = _outproj(xl, mod_l, uc_l, c2_l, z_l, skip2, gate_l, hf_l, hb_l, yw_l, yg_l, w_out4, n)

        if not last:
            spec_c = _filter_spectrum(
                _hyena_filter(nc, hy_w1[l], hy_b1[l], hy_freq[l], hy_w2[l], hy_b2[l], hy_w3[l], d_hy), tabs_c)
            cc1 = _long_conv(uc_c[:, :d_hy], spec_c, 0, tabs_c)
            z_c = _hyena_gate(uc_c, 1, cc1, uc_c, 0, skip1)
            c2_c = _long_conv(z_c, spec_c, 1, tabs_c)
            yw_c = _dense_attention(qw_c, kw_c, vw_c, sink_rows, batch, nc)
            yg_c = _dense_attention(qg_c, kg_c, vg_c, no_sink, batch, nc)
            xc = _outproj(xc, mod_c, uc_c, c2_c, z_c, skip2, gate_c, hf_c, hb_c, yw_c, yg_c, w_out4, batch * nc)

        xl = _ffn(xl, mod_l, g2, *f2, rows_per_group=n, mod_base=6)
        if not last:
            xc = _ffn(xc, mod_c, g2, *f2, rows_per_group=batch * nc, mod_base=6)

    return _rmsnorm(xl, final_g.reshape(1, d)).reshape(batch, n, d)
```

```python
import functools
import math

import numpy as np
import jax
import jax.numpy as jnp
from jax import lax
from jax.experimental import pallas as pl
from jax.experimental.pallas import tpu as pltpu

F32 = jnp.float32
BF16 = jnp.bfloat16

HEAD_DIM = 64
GRID_W = 64
N_MOD = 9
WINDOW = 128
Q_BLOCK = 128
HY_ORDER = 2
HY_BANDS = 16
HY_MIN_DECAY = math.log(1e-2) / 1.5
HY_MAX_DECAY = math.log(1e-2) / 0.3
LRU_BLOCKS = 4
LRU_C = 8.0
ROPE_BASE = 10000.0
EPS = 1e-6
NEG_INF = -1e30
LOG2E = math.log2(math.e)
Q_SCALE = HEAD_DIM ** -0.5 * LOG2E

LANES = 128
SUBLANES = 8
VMEM_LIMIT = 56 * 1024 * 1024

TOKEN_TILE = 512
FF_CHUNK = 256
CONV_TILE = 256
SCAN_TILE = 256
FLASH_TQ = 256
FLASH_TK = 512
WIN_TQ = 512
FFT_COL_TILE = 1024


def _params(sem, vmem=None):
    return pltpu.CompilerParams(dimension_semantics=sem, vmem_limit_bytes=vmem)


def _split_bf16(x):
    hi = x.astype(BF16)
    lo = (x - hi.astype(F32)).astype(BF16)
    return hi, lo


def _dot(a, b):
    return jnp.dot(a, b, preferred_element_type=F32)


def _dot_nt(a, b):
    return lax.dot_general(a, b, (((1,), (1,)), ((), ())), preferred_element_type=F32)


def _dot3_const(ch, cl, x):
    xh, xl = _split_bf16(x)
    return _dot(ch, xh) + (_dot(cl, xh) + _dot(ch, xl))


def _mod_kernel(c_ref, w_ref, b_ref, o_ref):
    c = c_ref[...]
    a = c * jax.nn.sigmoid(c)
    ah, al = _split_bf16(a)
    wh, wl = _split_bf16(w_ref[0])
    o_ref[0] = _dot(ah, wh) + (_dot(al, wh) + _dot(ah, wl)) + b_ref[0]


def _modulation(cond, w_mod, b_mod):
    depth, d, nd = w_mod.shape
    tn = 1152
    assert nd % tn == 0
    return pl.pallas_call(
        _mod_kernel,
        grid=(depth, nd // tn),
        in_specs=[
            pl.BlockSpec((SUBLANES, d), lambda l, j: (0, 0)),
            pl.BlockSpec((1, d, tn), lambda l, j: (l, 0, j)),
            pl.BlockSpec((1, 1, tn), lambda l, j: (l, 0, j)),
        ],
        out_specs=pl.BlockSpec((1, SUBLANES, tn), lambda l, j: (l, 0, j)),
        out_shape=jax.ShapeDtypeStruct((depth, SUBLANES, nd), F32),
        compiler_params=_params(("parallel", "parallel")),
        name="modulation",
    )(cond, w_mod, b_mod.reshape(depth, 1, nd))


def _adaln(x, g, shift, scale):
    ms = jnp.mean(x * x, axis=-1, keepdims=True)
    y = x * lax.rsqrt(ms + EPS) * g
    return y * (1.0 + scale) + shift


def _group_map(rows_per_group, tile):
    tiles = rows_per_group // tile
    return lambda i: (i // tiles, 0, 0)


def _ffn_kernel(x_ref, mod_ref, g_ref, w1_ref, w3_ref, w2_ref, o_ref, xn_ref, acc_ref, *, mod_base, n_chunks):
    x = x_ref[...]
    shift = mod_ref[0, mod_base:mod_base + 1, :]
    scale = mod_ref[0, mod_base + 1:mod_base + 2, :]
    gate = mod_ref[0, mod_base + 2:mod_base + 3, :]
    xn_ref[...] = _adaln(x, g_ref[...], shift, scale).astype(BF16)
    acc_ref[...] = jnp.zeros_like(acc_ref)

    def body(c, carry):
        xn = xn_ref[...]
        h = _dot(xn, w1_ref[c])
        u = _dot(xn, w3_ref[c])
        a = (h * jax.nn.sigmoid(h) * u).astype(BF16)
        acc_ref[...] += _dot(a, w2_ref[c])
        return carry

    lax.fori_loop(0, n_chunks, body, 0, unroll=True)
    o_ref[...] = x + 0.5 * gate * acc_ref[...]


def _ffn(x, mod, g, w1c, w3c, w2c, rows_per_group, mod_base):
    rows, d = x.shape
    tm = TOKEN_TILE
    assert rows % tm == 0 and rows_per_group % tm == 0
    n_chunks, _, ch = w1c.shape
    const3 = lambda i: (0, 0, 0)
    return pl.pallas_call(
        functools.partial(_ffn_kernel, mod_base=mod_base, n_chunks=n_chunks),
        grid=(rows // tm,),
        in_specs=[
            pl.BlockSpec((tm, d), lambda i: (i, 0)),
            pl.BlockSpec((1, N_MOD, d), _group_map(rows_per_group, tm)),
            pl.BlockSpec((1, d), lambda i: (0, 0)),
            pl.BlockSpec((n_chunks, d, ch), const3, pipeline_mode=pl.Buffered(1)),
            pl.BlockSpec((n_chunks, d, ch), const3, pipeline_mode=pl.Buffered(1)),
            pl.BlockSpec((n_chunks, ch, d), const3, pipeline_mode=pl.Buffered(1)),
        ],
        out_specs=pl.BlockSpec((tm, d), lambda i: (i, 0)),
        out_shape=jax.ShapeDtypeStruct((rows, d), F32),
        scratch_shapes=[pltpu.VMEM((tm, d), BF16), pltpu.VMEM((tm, d), F32)],
        compiler_params=_params(("parallel",), VMEM_LIMIT),
        name="ffn",
    )(x, mod, g, w1c, w3c, w2c)


def _ffn_weights(w13, w2):
    d, two_ff = w13.shape
    ff = two_ff // 2
    assert ff % FF_CHUNK == 0
    nc = ff // FF_CHUNK
    w1c = w13[:, :ff].astype(BF16).reshape(d, nc, FF_CHUNK).transpose(1, 0, 2)
    w3c = w13[:, ff:].astype(BF16).reshape(d, nc, FF_CHUNK).transpose(1, 0, 2)
    w2c = w2.astype(BF16).reshape(nc, FF_CHUNK, d)
    return w1c, w3c, w2c


def _swap_halves(z):
    lane = lax.broadcasted_iota(jnp.int32, z.shape, 1)
    first = (lane % HEAD_DIM) < (HEAD_DIM // 2)
    return jnp.where(first, pltpu.roll(z, LANES - HEAD_DIM // 2, 1), pltpu.roll(z, HEAD_DIM // 2, 1))


def _mixin_kernel(*refs, use_rope, d_hy3, d_lru, d_q, d_kv):
    if use_rope:
        (x_ref, mod_ref, g_ref, w_ref, gq_ref, gk_ref, hs_ref, cos_ref, sin_ref,
         uin_ref, gate_ref, qw_ref, kw_ref, vw_ref, qg_ref, kg_ref, vg_ref) = refs
    else:
        (x_ref, mod_ref, g_ref, w_ref, gq_ref, gk_ref, hs_ref,
         uin_ref, gate_ref, qw_ref, kw_ref, vw_ref, qg_ref, kg_ref, vg_ref) = refs
        cos_ref = sin_ref = None
    x = x_ref[...]
    xn = _adaln(x, g_ref[...], mod_ref[0, 3:4, :], mod_ref[0, 4:5, :]).astype(BF16)

    def proj(lo, width):
        return _dot(xn, w_ref[:, lo:lo + width])

    def rope(z):
        if not use_rope:
            return z
        return z * cos_ref[...] + _swap_halves(z) * sin_ref[...]

    def head_norm(z, gain):
        z2 = z * z
        zh, zl = _split_bf16(z2)
        ms = _dot(zh, hs_ref[...]) + _dot(zl, hs_ref[...])
        return z * lax.rsqrt(ms + EPS) * gain

    def store_values(v_ref, v):
        assert d_kv == LANES
        low = lax.broadcasted_iota(jnp.int32, v.shape, 1) < HEAD_DIM
        v_ref[:, 0:LANES] = jnp.where(low, v, 1.0).astype(BF16)
        v_ref[:, LANES:2 * LANES] = jnp.where(low, pltpu.roll(v, HEAD_DIM, 1), 1.0).astype(BF16)

    off = 0
    uin_ref[...] = proj(off, d_hy3 + d_lru)
    off += d_hy3 + d_lru
    gate_ref[...] = proj(off, d_lru)
    off += d_lru
    scale = Q_SCALE
    z = proj(off, d_q)
    for c in range(d_q // LANES):
        qw_ref[:, c * LANES:(c + 1) * LANES] = (rope(z[:, c * LANES:(c + 1) * LANES]) * scale).astype(BF16)
    off += d_q
    z = proj(off, 2 * d_kv)
    kw_ref[...] = rope(z[:, :d_kv]).astype(BF16)
    store_values(vw_ref, z[:, d_kv:])
    off += 2 * d_kv
    z = proj(off, d_q)
    for c in range(d_q // LANES):
        zn = head_norm(z[:, c * LANES:(c + 1) * LANES], gq_ref[...])
        qg_ref[:, c * LANES:(c + 1) * LANES] = (rope(zn) * scale).astype(BF16)
    off += d_q
    z = proj(off, 2 * d_kv)
    kg_ref[...] = rope(head_norm(z[:, :d_kv], gk_ref[...])).astype(BF16)
    store_values(vg_ref, z[:, d_kv:])


def _mixin(x, mod, g, w_in, gq, gk, hsum, rope_tabs, rows_per_group, seq_len, dims):
    rows, d = x.shape
    d_hy3, d_lru, d_q, d_kv = dims
    tm = TOKEN_TILE
    assert rows % tm == 0 and rows_per_group % tm == 0
    use_rope = rope_tabs is not None
    d_in = w_in.shape[1]
    row = lambda i: (i, 0)
    const2 = lambda i: (0, 0)
    in_specs = [
        pl.BlockSpec((tm, d), row),
        pl.BlockSpec((1, N_MOD, d), _group_map(rows_per_group, tm)),
        pl.BlockSpec((1, d), const2),
        pl.BlockSpec((d, d_in), const2, pipeline_mode=pl.Buffered(1)),
        pl.BlockSpec((1, LANES), const2),
        pl.BlockSpec((1, LANES), const2),
        pl.BlockSpec((LANES, LANES), const2),
    ]
    args = [x, mod, g, w_in, gq, gk, hsum]
    if use_rope:
        assert seq_len % tm == 0
        tiles_per_seq = seq_len // tm
        tab = lambda i: (i % tiles_per_seq, 0)
        in_specs += [pl.BlockSpec((tm, LANES), tab), pl.BlockSpec((tm, LANES), tab)]
        args += list(rope_tabs)
    widths = (d_hy3 + d_lru, d_lru, d_q, d_kv, 2 * d_kv, d_q, d_kv, 2 * d_kv)
    dtypes = (F32, F32, BF16, BF16, BF16, BF16, BF16, BF16)
    return pl.pallas_call(
        functools.partial(_mixin_kernel, use_rope=use_rope, d_hy3=d_hy3, d_lru=d_lru, d_q=d_q, d_kv=d_kv),
        grid=(rows // tm,),
        in_specs=in_specs,
        out_specs=[pl.BlockSpec((tm, w), row) for w in widths],
        out_shape=[jax.ShapeDtypeStruct((rows, w), dt) for w, dt in zip(widths, dtypes)],
        compiler_params=_params(("parallel",), VMEM_LIMIT),
        name="mixin_rope" if use_rope else "mixin",
    )(*args)


def _shortconv_kernel(cur_ref, prev_ref, next_ref, w_ref, b_ref, o_ref, ext_ref, *, tiles_per_seq):
    ts = cur_ref.shape[0]
    j = pl.program_id(0) % tiles_per_seq
    keep_prev = (j > 0).astype(F32)
    keep_next = (j < tiles_per_seq - 1).astype(F32)
    ext_ref[0:SUBLANES, :] = prev_ref[...] * keep_prev
    ext_ref[SUBLANES:SUBLANES + ts, :] = cur_ref[...]
    ext_ref[SUBLANES + ts:, :] = next_ref[...] * keep_next
    acc = b_ref[...] + w_ref[2:3, :] * cur_ref[...]
    for k in (0, 1, 3):
        acc = acc + w_ref[k:k + 1, :] * ext_ref[SUBLANES - 2 + k:SUBLANES - 2 + k + ts, :]
    o_ref[...] = acc


def _shortconv(u, w4, bias, seq_len):
    rows, ch = u.shape
    ts = min(CONV_TILE, seq_len)
    assert seq_len % ts == 0 and rows % ts == 0 and ts % SUBLANES == 0
    tiles_per_seq = seq_len // ts
    sub = ts // SUBLANES
    last = rows // SUBLANES - 1
    return pl.pallas_call(
        functools.partial(_shortconv_kernel, tiles_per_seq=tiles_per_seq),
        grid=(rows // ts,),
        in_specs=[
            pl.BlockSpec((ts, ch), lambda i: (i, 0)),
            pl.BlockSpec((SUBLANES, ch), lambda i: (jnp.maximum(i * sub - 1, 0), 0)),
            pl.BlockSpec((SUBLANES, ch), lambda i: (jnp.minimum((i + 1) * sub, last), 0)),
            pl.BlockSpec((4, ch), lambda i: (0, 0)),
            pl.BlockSpec((1, ch), lambda i: (0, 0)),
        ],
        out_specs=pl.BlockSpec((ts, ch), lambda i: (i, 0)),
        out_shape=jax.ShapeDtypeStruct((rows, ch), F32),
        scratch_shapes=[pltpu.VMEM((ts + 2 * SUBLANES, ch), F32)],
        compiler_params=_params(("parallel",)),
        name="shortconv",
    )(u, u, u, w4, bias)


def _fft_split(n):
    big = 2 * n
    n2 = LANES if big >= 16 * LANES else LANES // 4
    assert big % n2 == 0 and (big // n2) % 16 == 0
    return big // n2, n2


def _np_hi_lo(m):
    m32 = jnp.asarray(np.asarray(m, dtype=np.float32))
    hi = m32.astype(BF16)
    lo = (m32 - hi.astype(F32)).astype(BF16)
    return hi, lo


@functools.lru_cache(maxsize=None)
def _fft_tables(n):
    n1, n2 = _fft_split(n)
    big = n1 * n2
    h1 = n1 // 2
    f1 = np.arange(n1)[:, None].astype(np.float64)
    s1 = np.arange(h1)[None, :].astype(np.float64)
    th = 2.0 * np.pi * f1 * s1 / n1
    c, s = np.cos(th), np.sin(th)
    w1 = np.block([[c, s], [-s, c]])
    w3 = np.block([[c.T, -s.T], [s.T, c.T]]) / big
    s1f = np.arange(n1)[None, :].astype(np.float64)
    thf = 2.0 * np.pi * f1 * s1f / n1
    w1f = np.concatenate([np.cos(thf), -np.sin(thf)], axis=0)
    f2 = np.arange(n2)[:, None].astype(np.float64)
    s2 = np.arange(n2)[None, :].astype(np.float64)
    th2 = 2.0 * np.pi * f2 * s2 / n2
    eye = np.eye(LANES // n2)
    c2, sn2 = np.kron(eye, np.cos(th2)), np.kron(eye, np.sin(th2))
    m2 = np.block([[c2, -sn2], [sn2, c2]])
    ph = 2.0 * np.pi * np.arange(n1)[:, None] * np.arange(n2)[None, :] / big
    tw = np.stack([np.cos(ph), -np.sin(ph)], axis=0)
    tw = np.tile(tw, (1, 1, LANES // n2)).astype(np.float32)
    return dict(n1=n1, n2=n2, w1=w1, w3=w3, w1f=w1f, m2=m2, m2t=m2.T, tw=tw)


def _dot3_rconst(x, ch, cl):
    xh, xl = _split_bf16(x)
    return _dot(xh, ch) + (_dot(xl, ch) + _dot(xh, cl))


def _fftconv_kernel(*refs, fwd_only, n1, groups):
    if fwd_only:
        v_ref, tw_ref, w1h_ref, w1l_ref, mh_ref, ml_ref, o_ref = refs
    else:
        (v_ref, sr_ref, si_ref, tw_ref, w1h_ref, w1l_ref, mh_ref, ml_ref, mth_ref, mtl_ref,
         w3h_ref, w3l_ref, o_ref) = refs

    def stack(x):
        return jnp.concatenate([x[:, g * LANES:(g + 1) * LANES] for g in range(groups)], axis=0)

    def unstack(x):
        return jnp.concatenate([x[g * n1:(g + 1) * n1] for g in range(groups)], axis=1)

    a = _dot3_const(w1h_ref[...], w1l_ref[...], v_ref[...])
    twr = jnp.concatenate([tw_ref[0]] * groups, axis=0)
    twi = jnp.concatenate([tw_ref[1]] * groups, axis=0)
    ar, ai = stack(a[:n1]), stack(a[n1:])
    b = jnp.concatenate([ar * twr - ai * twi, ar * twi + ai * twr], axis=1)
    x = _dot3_rconst(b, mh_ref[...], ml_ref[...])
    xr, xi = x[:, :LANES], x[:, LANES:]
    if fwd_only:
        o_ref[0:n1, :] = unstack(xr)
        o_ref[n1:2 * n1, :] = unstack(xi)
        return
    kr, ki = stack(sr_ref[...]), stack(si_ref[...])
    y = jnp.concatenate([xr * kr - xi * ki, xr * ki + xi * kr], axis=1)
    c = _dot3_rconst(y, mth_ref[...], mtl_ref[...])
    cr, ci = c[:, :LANES], c[:, LANES:]
    d = jnp.concatenate([unstack(cr * twr + ci * twi), unstack(ci * twr - cr * twi)], axis=0)
    o_ref[...] = _dot3_const(w3h_ref[...], w3l_ref[...], d)


def _fftconv(v2, spec, order, tabs, fwd_only):
    n1 = tabs["n1"]
    cols = v2.shape[1]
    tc = min(FFT_COL_TILE, cols)
    assert cols % tc == 0 and v2.shape[0] == n1
    nblk = cols // tc
    const2 = lambda j: (0, 0)
    const3 = lambda j: (0, 0, 0)
    w1h, w1l = _np_hi_lo(tabs["w1f"] if fwd_only else tabs["w1"])
    mh, ml = _np_hi_lo(tabs["m2"])
    in_specs = [pl.BlockSpec((n1, tc), lambda j: (0, j))]
    args = [v2]
    if not fwd_only:
        in_specs += [pl.BlockSpec((n1, tc), lambda j: (0, order * nblk + j)),
                     pl.BlockSpec((n1, tc), lambda j: (1, order * nblk + j))]
        args += [spec, spec]
    in_specs += [pl.BlockSpec((2, n1, LANES), const3), pl.BlockSpec(w1h.shape, const2), pl.BlockSpec(w1h.shape, const2),
                 pl.BlockSpec(mh.shape, const2), pl.BlockSpec(mh.shape, const2)]
    args += [jnp.asarray(tabs["tw"]), w1h, w1l, mh, ml]
    if not fwd_only:
        mth, mtl = _np_hi_lo(tabs["m2t"])
        w3h, w3l = _np_hi_lo(tabs["w3"])
        in_specs += [pl.BlockSpec(mh.shape, const2), pl.BlockSpec(mh.shape, const2),
                     pl.BlockSpec(w3h.shape, const2), pl.BlockSpec(w3h.shape, const2)]
        args += [mth, mtl, w3h, w3l]
    out_rows = 2 * n1 if fwd_only else n1
    return pl.pallas_call(
        functools.partial(_fftconv_kernel, fwd_only=fwd_only, n1=n1, groups=tc // LANES),
        grid=(nblk,),
        in_specs=in_specs,
        out_specs=pl.BlockSpec((out_rows, tc), lambda j: (0, j)),
        out_shape=jax.ShapeDtypeStruct((out_rows, cols), F32),
        compiler_params=_params(("parallel",), VMEM_LIMIT),
        name="fft_filter" if fwd_only else "fft_conv",
    )(*args)


def _to_fft_layout(v, n1, n2):
    ch = v.shape[1]
    return v.reshape(n1, n2, ch).transpose(0, 2, 1).reshape(n1, ch * n2)


def _from_fft_layout(y, n2, ch):
    rows = y.shape[0]
    return y.reshape(rows, ch, n2).transpose(0, 2, 1).reshape(rows * n2, ch)


def _filter_spectrum(k, tabs):
    return _fftconv(_to_fft_layout(k, tabs["n1"], tabs["n2"]), None, 0, tabs, True)


def _long_conv(v, spec, order, tabs):
    n1, n2 = tabs["n1"], tabs["n2"]
    y = _fftconv(_to_fft_layout(v, n1, n2), spec, order, tabs, False)
    return _from_fft_layout(y, n2, v.shape[1])


def _hyena_filter(n, w1, b1, freq, w2, b2, w3, d_hy):
    hp = lax.Precision.HIGHEST
    r = jnp.arange(2 * n)
    pos = jnp.where(r < n, r, jnp.where(r == n, 0, 2 * n - r))
    t = jnp.linspace(0.0, 1.0, n, dtype=F32)[pos][:, None]
    w = 2.0 * math.pi * pos.astype(F32)[:, None] / n
    f = jnp.linspace(1e-4, HY_BANDS - 1, HY_BANDS, dtype=F32)[None, :]
    z = jnp.concatenate([t, jnp.cos(f * w), -jnp.sin(f * w)], axis=-1)
    h = jnp.sin(freq[0] * (jnp.dot(z, w1, precision=hp) + b1))
    h = jnp.sin(freq[1] * (jnp.dot(h, w2, precision=hp) + b2))
    h = jnp.dot(h, w3, precision=hp).astype(F32).reshape(2 * n, HY_ORDER, 2, d_hy)
    h = jnp.where((r < n)[:, None, None], h[:, :, 0], h[:, :, 1])
    deltas = jnp.abs(jnp.linspace(HY_MIN_DECAY, HY_MAX_DECAY, d_hy, dtype=F32))
    h = h * jnp.exp(-t[:, :, None] * deltas)
    norm = jnp.sum(jnp.abs(h), axis=0, keepdims=True) + EPS
    k = jnp.where((r == n)[:, None, None], 0.0, h) / norm
    return k.reshape(2 * n, HY_ORDER * d_hy)


def _gate_kernel(a_ref, b_ref, c_ref, s_ref, o_ref):
    o_ref[...] = a_ref[...] * (b_ref[...] + c_ref[...] * s_ref[...])


def _hyena_gate(uc, a_col, conv, c_src, c_col, skip):
    rows, ch = conv.shape
    tm = min(TOKEN_TILE, rows)
    assert rows % tm == 0
    return pl.pallas_call(
        _gate_kernel,
        grid=(rows // tm,),
        in_specs=[
            pl.BlockSpec((tm, ch), lambda i: (i, a_col)),
            pl.BlockSpec((tm, ch), lambda i: (i, 0)),
            pl.BlockSpec((tm, ch), lambda i: (i, c_col)),
            pl.BlockSpec((1, ch), lambda i: (0, 0)),
        ],
        out_specs=pl.BlockSpec((tm, ch), lambda i: (i, 0)),
        out_shape=jax.ShapeDtypeStruct((rows, ch), F32),
        compiler_params=_params(("parallel",)),
        name="hyena_gate",
    )(uc, conv, c_src, skip)


def _shift_rows(x, s, fill, reverse):
    t = x.shape[0]
    row = lax.broadcasted_iota(jnp.int32, x.shape, 0)
    if reverse:
        return jnp.where(row >= t - s, fill, pltpu.roll(x, t - s, 0))
    return jnp.where(row < s, fill, pltpu.roll(x, s, 0))


def _lru_kernel(uf_ref, ub_ref, wa_ref, wx_ref, p_ref, h0_ref, hf_ref, hb_ref, hl_ref, carry_ref):
    j = pl.program_id(1)
    nt = pl.num_programs(1)
    tl = uf_ref.shape[0]

    @pl.when(j == 0)
    def _():
        carry_ref[...] = h0_ref[0]

    for d, (u_ref, o_ref) in enumerate(((uf_ref, hf_ref), (ub_ref, hb_ref))):
        reverse = d == 1
        u = u_ref[...]
        ub16 = u.astype(BF16)
        r = jax.nn.sigmoid(_dot(ub16, wa_ref[d]) + p_ref[d, 0:1, :])
        gi = jax.nn.sigmoid(_dot(ub16, wx_ref[d]) + p_ref[d, 1:2, :])
        nlam = -p_ref[d, 2:3, :]
        softplus = jnp.maximum(nlam, 0.0) + jnp.log1p(jnp.exp(-jnp.abs(nlam)))
        log_a = -LRU_C * r * softplus
        a = jnp.exp(log_a)
        b = jnp.sqrt(-jnp.tanh(log_a) * (a * a + 1.0)) * gi * u
        s = 1
        while s < tl:
            a_s = _shift_rows(a, s, 1.0, reverse)
            b_s = _shift_rows(b, s, 0.0, reverse)
            b = a * b_s + b
            a = a * a_s
            s *= 2
        h = b + a * carry_ref[d:d + 1, :]
        o_ref[...] = h
        carry_ref[d:d + 1, :] = h[0:1, :] if reverse else h[tl - 1:tl, :]

    @pl.when(j == nt - 1)
    def _():
        hl_ref[0] = carry_ref[...]


def _lru(uc, col, wa, wx, p, h0, batch, seq_len):
    ch = wa.shape[-1]
    tl = min(SCAN_TILE, seq_len)
    assert seq_len % tl == 0
    nt = seq_len // tl
    rows = batch * seq_len
    fwd = lambda b, j: (b * nt + j, col)
    bwd = lambda b, j: (b * nt + nt - 1 - j, col)
    fwd_o = lambda b, j: (b * nt + j, 0)
    bwd_o = lambda b, j: (b * nt + nt - 1 - j, 0)
    const3 = lambda b, j: (0, 0, 0)
    return pl.pallas_call(
        _lru_kernel,
        grid=(batch, nt),
        in_specs=[
            pl.BlockSpec((tl, ch), fwd),
            pl.BlockSpec((tl, ch), bwd),
            pl.BlockSpec((2, ch, ch), const3),
            pl.BlockSpec((2, ch, ch), const3),
            pl.BlockSpec((2, 3, ch), const3),
            pl.BlockSpec((1, 2, ch), lambda b, j: (b, 0, 0)),
        ],
        out_specs=[
            pl.BlockSpec((tl, ch), fwd_o),
            pl.BlockSpec((tl, ch), bwd_o),
            pl.BlockSpec((1, 2, ch), lambda b, j: (b, 0, 0)),
        ],
        out_shape=[
            jax.ShapeDtypeStruct((rows, ch), F32),
            jax.ShapeDtypeStruct((rows, ch), F32),
            jax.ShapeDtypeStruct((batch, 2, ch), F32),
        ],
        scratch_shapes=[pltpu.VMEM((2, ch), F32)],
        compiler_params=_params(("parallel", "arbitrary")),
        name="rglru",
    )(uc, uc, wa, wx, p, h0)


def _block_diag(w):
    two, nb, d, _ = w.shape
    eye = jnp.eye(nb, dtype=w.dtype)
    return jnp.einsum('xnde,nm->xndme', w, eye).reshape(two, nb * d, nb * d)


def _expand_q(q_ref, qe_ref):
    tq = q_ref.shape[0]
    half = LANES // 2
    low = lax.broadcasted_iota(jnp.int32, (tq, LANES), 1) < half
    for kv in range(2):
        qf = q_ref[:, kv * LANES:(kv + 1) * LANES].astype(F32)
        qr = pltpu.roll(qf, half, 1)
        if kv == 0:
            g0, g1 = jnp.where(low, qf, 0.0), jnp.where(low, qr, 0.0)
        else:
            g0, g1 = jnp.where(low, 0.0, qr), jnp.where(low, 0.0, qf)
        qe_ref[kv, 0:tq, :] = g0.astype(BF16)
        qe_ref[kv, tq:2 * tq, :] = g1.astype(BF16)


def _row_max(s):
    m = s[:, 0:LANES]
    for c in range(1, s.shape[1] // LANES):
        m = jnp.maximum(m, s[:, c * LANES:(c + 1) * LANES])
    return jnp.max(m, axis=-1, keepdims=True)


def _exp2_bf16(s, m):
    return jnp.concatenate(
        [jnp.exp2(s[:, c * LANES:(c + 1) * LANES] - m).astype(BF16) for c in range(s.shape[1] // LANES)], axis=1)


def _finish_heads(acc, tq):
    half = LANES // 2
    o = acc / pltpu.roll(acc, half, 1)
    low = lax.broadcasted_iota(jnp.int32, (tq, LANES), 1) < half
    return jnp.where(low, o[0:tq], pltpu.roll(o[tq:2 * tq], half, 1))


def _win_kernel(q_ref, kp_ref, kc_ref, kn_ref, vp_ref, vc_ref, vn_ref, kx_ref, vx_ref, sink_ref, o_ref, qe_ref,
                *, seq_len):
    i = pl.program_id(1)
    tq = q_ref.shape[0]
    halo = kp_ref.shape[0]
    _expand_q(q_ref, qe_ref)
    kk = jnp.concatenate([kp_ref[...], kc_ref[...], kn_ref[...]], axis=0)
    vv = jnp.concatenate([vp_ref[...], vc_ref[...], vn_ref[...]], axis=0)
    n_loc = tq + 2 * halo
    row = lax.broadcasted_iota(jnp.int32, (tq, n_loc), 0)
    col = lax.broadcasted_iota(jnp.int32, (tq, n_loc), 1)
    kpos = i * tq - halo + col
    valid = (jnp.abs(col - halo - row) <= WINDOW) & (kpos >= 0) & (kpos < seq_len)
    valid = jnp.concatenate([valid, valid], axis=0)
    high = lax.broadcasted_iota(jnp.int32, (2 * tq, LANES), 1) >= LANES // 2
    for kv in range(2):
        vsl = slice(kv * LANES, (kv + 1) * LANES)
        s_loc = jnp.where(valid, _dot_nt(qe_ref[kv], kk), NEG_INF)
        s_ctx = _dot_nt(qe_ref[kv], kx_ref[...])
        sink = jnp.concatenate([jnp.broadcast_to(sink_ref[2 * kv + g:2 * kv + g + 1, :], (tq, LANES))
                                for g in range(2)], axis=0)
        m = jnp.maximum(jnp.maximum(_row_max(s_loc), _row_max(s_ctx)), sink)
        acc = _dot(_exp2_bf16(s_loc, m), vv[:, vsl]) + _dot(_exp2_bf16(s_ctx, m), vx_ref[:, vsl])
        acc = acc + jnp.where(high, jnp.exp2(sink - m), 0.0)
        o_ref[:, vsl] = _finish_heads(acc, tq).astype(o_ref.dtype)


def _window_attention(q, k, v_ext, kx, vx_ext, sink_rows, batch, seq_len, ctx_len):
    tq = min(WIN_TQ, seq_len)
    halo = Q_BLOCK
    assert seq_len % tq == 0 and tq % halo == 0 and WINDOW <= halo
    nb = seq_len // tq
    per = tq // halo
    last = seq_len // halo - 1
    dq, dk, dv = q.shape[1], k.shape[1], v_ext.shape[1]
    assert dk == LANES and dq == 2 * LANES and dv == 2 * LANES
    cur = lambda b, i: (b * nb + i, 0)
    prv = lambda b, i: (b * (last + 1) + jnp.maximum(i * per - 1, 0), 0)
    nxt = lambda b, i: (b * (last + 1) + jnp.minimum((i + 1) * per, last), 0)
    cx = lambda b, i: (b, 0)
    return pl.pallas_call(
        functools.partial(_win_kernel, seq_len=seq_len),
        grid=(batch, nb),
        in_specs=[
            pl.BlockSpec((tq, dq), cur),
            pl.BlockSpec((halo, dk), prv), pl.BlockSpec((tq, dk), cur), pl.BlockSpec((halo, dk), nxt),
            pl.BlockSpec((halo, dv), prv), pl.BlockSpec((tq, dv), cur), pl.BlockSpec((halo, dv), nxt),
            pl.BlockSpec((ctx_len, dk), cx), pl.BlockSpec((ctx_len, dv), cx),
            pl.BlockSpec(sink_rows.shape, lambda b, i: (0, 0)),
        ],
        out_specs=pl.BlockSpec((tq, dq), cur),
        out_shape=jax.ShapeDtypeStruct((batch * seq_len, dq), BF16),
        scratch_shapes=[pltpu.VMEM((2, 2 * tq, LANES), BF16)],
        compiler_params=_params(("parallel", "parallel"), VMEM_LIMIT),
        name="window_attention",
    )(q, k, k, k, v_ext, v_ext, v_ext, kx, vx_ext, sink_rows)


def _dense_kernel(q_ref, k_ref, v_ref, sink_ref, o_ref, qe_ref):
    tq = q_ref.shape[0]
    _expand_q(q_ref, qe_ref)
    high = lax.broadcasted_iota(jnp.int32, (2 * tq, LANES), 1) >= LANES // 2
    for kv in range(2):
        vsl = slice(kv * LANES, (kv + 1) * LANES)
        s = _dot_nt(qe_ref[kv], k_ref[...])
        sink = jnp.concatenate([jnp.broadcast_to(sink_ref[2 * kv + g:2 * kv + g + 1, :], (tq, LANES))
                                for g in range(2)], axis=0)
        m = jnp.maximum(_row_max(s), sink)
        acc = _dot(_exp2_bf16(s, m), v_ref[:, vsl]) + jnp.where(high, jnp.exp2(sink - m), 0.0)
        o_ref[:, vsl] = _finish_heads(acc, tq).astype(o_ref.dtype)


def _dense_attention(q, k, v_ext, sink_rows, batch, seq_len):
    dq, dk, dv = q.shape[1], k.shape[1], v_ext.shape[1]
    assert dk == LANES and dq == 2 * LANES and dv == 2 * LANES
    blk = lambda b: (b, 0)
    return pl.pallas_call(
        _dense_kernel,
        grid=(batch,),
        in_specs=[pl.BlockSpec((seq_len, dq), blk), pl.BlockSpec((seq_len, dk), blk), pl.BlockSpec((seq_len, dv), blk),
                  pl.BlockSpec(sink_rows.shape, lambda b: (0, 0))],
        out_specs=pl.BlockSpec((seq_len, dq), blk),
        out_shape=jax.ShapeDtypeStruct((batch * seq_len, dq), BF16),
        scratch_shapes=[pltpu.VMEM((2, 2 * seq_len, LANES), BF16)],
        compiler_params=_params(("parallel",)),
        name="dense_attention",
    )(q, k, v_ext, sink_rows)


def _flash_kernel(q_ref, k_ref, v_ref, kx_ref, vx_ref, o_ref, qe_ref, m_ref, acc_ref, s_ref, mx_ref, *, tk):
    tq = q_ref.shape[0]
    nb = k_ref.shape[0] // tk
    _expand_q(q_ref, qe_ref)
    m_ref[...] = jnp.full_like(m_ref, NEG_INF)
    acc_ref[...] = jnp.zeros_like(acc_ref)

    def update(kv, s, row_max, vblk):
        m_old = m_ref[kv]
        m_new = jnp.maximum(m_old, row_max)
        alpha = jnp.exp2(m_old - m_new)
        acc_ref[kv] = alpha * acc_ref[kv] + _dot(_exp2_bf16(s, m_new), vblk)
        m_ref[kv] = m_new

    for kv in range(2):
        s = _dot_nt(qe_ref[kv], kx_ref[...])
        update(kv, s, _row_max(s), vx_ref[:, kv * LANES:(kv + 1) * LANES])

    def scores(j, slot):
        kblk = k_ref[pl.ds(pl.multiple_of(j * tk, tk), tk), :]
        for kv in range(2):
            s = _dot_nt(qe_ref[kv], kblk)
            s_ref[slot, kv] = s
            mx_ref[slot, kv] = jnp.broadcast_to(_row_max(s), mx_ref.shape[2:])

    def consume(j, slot):
        off = pl.multiple_of(j * tk, tk)
        for kv in range(2):
            update(kv, s_ref[slot, kv], mx_ref[slot, kv], v_ref[pl.ds(off, tk), kv * LANES:(kv + 1) * LANES])

    scores(0, 0)

    def body(jj, carry):
        j = 2 * jj
        scores(j + 1, 1)
        consume(j, 0)
        scores(j + 2, 0)
        consume(j + 1, 1)
        return carry

    lax.fori_loop(0, nb // 2 - 1, body, 0)
    scores(nb - 1, 1)
    consume(nb - 2, 0)
    consume(nb - 1, 1)
    for kv in range(2):
        o_ref[:, kv * LANES:(kv + 1) * LANES] = _finish_heads(acc_ref[kv], tq).astype(o_ref.dtype)


def _global_attention(q, k, v_ext, kx, vx_ext, batch, seq_len, ctx_len):
    tq = min(FLASH_TQ, seq_len)
    tk = min(FLASH_TK, seq_len)
    assert seq_len % tq == 0 and seq_len % (2 * tk) == 0
    nq = seq_len // tq
    dq, dk, dv = q.shape[1], k.shape[1], v_ext.shape[1]
    assert dk == LANES and dq == 2 * LANES and dv == 2 * LANES
    qmap = lambda b, i: (b * nq + i, 0)
    whole = lambda b, i: (b, 0)
    return pl.pallas_call(
        functools.partial(_flash_kernel, tk=tk),
        grid=(batch, nq),
        in_specs=[
            pl.BlockSpec((tq, dq), qmap),
            pl.BlockSpec((seq_len, dk), whole), pl.BlockSpec((seq_len, dv), whole),
            pl.BlockSpec((ctx_len, dk), whole), pl.BlockSpec((ctx_len, dv), whole),
        ],
        out_specs=pl.BlockSpec((tq, dq), qmap),
        out_shape=jax.ShapeDtypeStruct((batch * seq_len, dq), BF16),
        scratch_shapes=[
            pltpu.VMEM((2, 2 * tq, LANES), BF16),
            pltpu.VMEM((2, 2 * tq, LANES), F32),
            pltpu.VMEM((2, 2 * tq, LANES), F32),
            pltpu.VMEM((2, 2, 2 * tq, tk), F32),
            pltpu.VMEM((2, 2, 2 * tq, LANES), F32),
        ],
        compiler_params=_params(("parallel", "parallel"), VMEM_LIMIT),
        name="global_attention",
    )(q, k, v_ext, kx, vx_ext)


def _outproj_kernel(x_ref, mod_ref, x2_ref, conv_ref, z_ref, skip_ref, gate_ref, hf_ref, hb_ref,
                    yw_ref, yg_ref, w_ref, o_ref):
    y_hy = x2_ref[...] * (conv_ref[...] + z_ref[...] * skip_ref[...])
    y_lru = jax.nn.gelu(gate_ref[...]) * (hf_ref[...] + hb_ref[...])
    acc = _dot(y_hy.astype(BF16), w_ref[0])
    acc = acc + _dot(y_lru.astype(BF16), w_ref[1])
    acc = acc + _dot(yw_ref[...], w_ref[2])
    acc = acc + _dot(yg_ref[...], w_ref[3])
    o_ref[...] = x_ref[...] + mod_ref[0, 5:6, :] * acc


def _outproj(x, mod, uc, conv2, z, skip2, gate, hf, hb, yw, yg, w_out4, rows_per_group):
    rows, d = x.shape
    tm = TOKEN_TILE
    assert rows % tm == 0 and rows_per_group % tm == 0
    ch = conv2.shape[1]
    row = lambda i: (i, 0)
    return pl.pallas_call(
        _outproj_kernel,
        grid=(rows // tm,),
        in_specs=[
            pl.BlockSpec((tm, d), row),
            pl.BlockSpec((1, N_MOD, d), _group_map(rows_per_group, tm)),
            pl.BlockSpec((tm, ch), lambda i: (i, 2)),
            pl.BlockSpec((tm, ch), row),
            pl.BlockSpec((tm, ch), row),
            pl.BlockSpec((1, ch), lambda i: (0, 0)),
            pl.BlockSpec((tm, ch), row),
            pl.BlockSpec((tm, ch), row),
            pl.BlockSpec((tm, ch), row),
            pl.BlockSpec((tm, ch), row),
            pl.BlockSpec((tm, ch), row),
            pl.BlockSpec(w_out4.shape, lambda i: (0, 0, 0)),
        ],
        out_specs=pl.BlockSpec((tm, d), row),
        out_shape=jax.ShapeDtypeStruct((rows, d), F32),
        compiler_params=_params(("parallel",), VMEM_LIMIT),
        name="outproj",
    )(x, mod, uc, conv2, z, skip2, gate, hf, hb, yw, yg, w_out4)


def _rmsnorm_kernel(x_ref, g_ref, o_ref):
    x = x_ref[...]
    ms = jnp.mean(x * x, axis=-1, keepdims=True)
    o_ref[...] = x * lax.rsqrt(ms + EPS) * g_ref[...]


def _rmsnorm(x, g):
    rows, d = x.shape
    tm = TOKEN_TILE
    return pl.pallas_call(
        _rmsnorm_kernel,
        grid=(rows // tm,),
        in_specs=[pl.BlockSpec((tm, d), lambda i: (i, 0)), pl.BlockSpec((1, d), lambda i: (0, 0))],
        out_specs=pl.BlockSpec((tm, d), lambda i: (i, 0)),
        out_shape=jax.ShapeDtypeStruct((rows, d), F32),
        compiler_params=_params(("parallel",)),
        name="final_norm",
    )(x, g)


def _rope_tables(n):
    rows = n // GRID_W
    n_freq = HEAD_DIM // 4
    row = jnp.repeat(jnp.arange(rows, dtype=F32), GRID_W, total_repeat_length=n)
    col = jnp.tile(jnp.arange(GRID_W, dtype=F32), rows)
    inv = ROPE_BASE ** (-jnp.arange(n_freq, dtype=F32) / n_freq)
    ang = jnp.concatenate([row[:, None] * inv, col[:, None] * inv], axis=-1)
    cos, sin = jnp.cos(ang), jnp.sin(ang)
    reps = LANES // HEAD_DIM
    cos_t = jnp.tile(jnp.concatenate([cos, cos], axis=-1), (1, reps))
    sin_t = jnp.tile(jnp.concatenate([-sin, sin], axis=-1), (1, reps))
    return cos_t, sin_t


def _lane_rows(vals):
    h = vals.shape[0]
    out = jnp.zeros((SUBLANES, LANES), F32)
    return out.at[:h].set(jnp.broadcast_to(vals.astype(F32)[:, None], (h, LANES)))


def kernel(x, c, ctx, c_ctx, w_mod, b_mod, norm_g, ffn1_w13, ffn1_w2, ffn2_w13, ffn2_w2, w_in, w_out,
           hy_conv_w, hy_conv_b, hy_w1, hy_b1, hy_freq, hy_w2, hy_b2, hy_w3, hy_skip,
           lru_conv_w, lru_conv_b, lru_wa, lru_ba, lru_wx, lru_bx, lru_lambda,
           win_sink, qk_gain, final_g):
    batch, n, d = x.shape
    nc = ctx.shape[1]
    depth = w_mod.shape[0]
    assert batch == 2, "the two batch entries are packed as one complex signal in the long convolution"
    d_hy = hy_skip.shape[-1]
    d_lru = lru_conv_w.shape[-1]
    n_heads_win = win_sink.shape[-1]
    d_q = n_heads_win * HEAD_DIM
    d_kv = d_q // 2
    dims = (3 * d_hy, d_lru, d_q, d_kv)
    assert d_hy == d_lru == d_q and d_hy % LANES == 0

    xl = x.reshape(batch * n, d)
    xc = ctx.reshape(batch * nc, d)

    cond = jnp.zeros((SUBLANES, d), F32).at[:batch].set(c).at[batch].set(c_ctx)
    mods = _modulation(cond, w_mod, b_mod).reshape(depth, SUBLANES, N_MOD, d)

    cos_t, sin_t = _rope_tables(n)
    hsum = jnp.asarray(np.kron(np.eye(LANES // HEAD_DIM), np.full((HEAD_DIM, HEAD_DIM), 1.0 / HEAD_DIM)), dtype=BF16)
    tabs_l = _fft_tables(n)
    tabs_c = _fft_tables(nc)
    no_sink = jnp.full((SUBLANES, LANES), NEG_INF, F32)

    for l in range(depth):
        last = l == depth - 1
        mod_l = mods[l, :batch]
        mod_c = mods[l, batch:batch + 1]
        f1 = _ffn_weights(ffn1_w13[l], ffn1_w2[l])
        f2 = _ffn_weights(ffn2_w13[l], ffn2_w2[l])
        g0, g1, g2 = (norm_g[l, i].reshape(1, d) for i in range(3))

        xl = _ffn(xl, mod_l, g0, *f1, rows_per_group=n, mod_base=0)
        xc = _ffn(xc, mod_c, g0, *f1, rows_per_group=batch * nc, mod_base=0)

        w_in_l = w_in[l].astype(BF16)
        gq = jnp.tile(qk_gain[l, 0], LANES // HEAD_DIM).reshape(1, LANES)
        gk = jnp.tile(qk_gain[l, 1], LANES // HEAD_DIM).reshape(1, LANES)
        pl_ = _mixin(xl, mod_l, g1, w_in_l, gq, gk, hsum, (cos_t, sin_t), n, n, dims)
        pc_ = _mixin(xc, mod_c, g1, w_in_l, gq, gk, hsum, None, batch * nc, nc, dims)
        uin_l, gate_l, qw_l, kw_l, vw_l, qg_l, kg_l, vg_l = pl_
        uin_c, gate_c, qw_c, kw_c, vw_c, qg_c, kg_c, vg_c = pc_

        w4 = jnp.concatenate(
            [jnp.concatenate([jnp.zeros((1, 3 * d_hy), F32), hy_conv_w[l]], axis=0), lru_conv_w[l]], axis=1)
        b4 = jnp.concatenate([hy_conv_b[l], lru_conv_b[l]]).reshape(1, -1)
        uc_l = _shortconv(uin_l, w4, b4, n)
        uc_c = _shortconv(uin_c, w4, b4, nc)

        skip1 = hy_skip[l, 0].reshape(1, d_hy)
        skip2 = hy_skip[l, 1].reshape(1, d_hy)
        spec_l = _filter_spectrum(
            _hyena_filter(n, hy_w1[l], hy_b1[l], hy_freq[l], hy_w2[l], hy_b2[l], hy_w3[l], d_hy), tabs_l)
        c1 = _long_conv(uc_l[:, :d_hy], spec_l, 0, tabs_l)
        z_l = _hyena_gate(uc_l, 1, c1, uc_l, 0, skip1)
        c2_l = _long_conv(z_l, spec_l, 1, tabs_l)

        wa = _block_diag(lru_wa[l]).astype(BF16)
        wx = _block_diag(lru_wx[l]).astype(BF16)
        lp = jnp.stack([lru_ba[l], lru_bx[l], lru_lambda[l]], axis=1)
        lru_col = 3 * d_hy // d_lru
        hf_c, hb_c, h_last = _lru(uc_c, lru_col, wa, wx, lp, jnp.zeros((batch, 2, d_lru), F32), batch, nc)
        hf_l, hb_l, _ = _lru(uc_l, lru_col, wa, wx, lp, h_last, batch, n)

        sink_rows = _lane_rows(win_sink[l] * LOG2E)
        yw_l = _window_attention(qw_l, kw_l, vw_l, kw_c, vw_c, sink_rows, batch, n, nc)

        yg_l = _global_attention(qg_l, kg_l, vg_l, kg_c, vg_c, batch, n, nc)

        w_out4 = w_out[l].astype(BF16).reshape(4, d_hy, d)
        xl = _outproj(xl, mod_l, uc_l, c2_l, z_l, skip2, gate_l, hf_l, hb_l, yw_l, yg_l, w_out4, n)

        if not last:
            spec_c = _filter_spectrum(
                _hyena_filter(nc, hy_w1[l], hy_b1[l], hy_freq[l], hy_w2[l], hy_b2[l], hy_w3[l], d_hy), tabs_c)
            cc1 = _long_conv(uc_c[:, :d_hy], spec_c, 0, tabs_c)
            z_c = _hyena_gate(uc_c, 1, cc1, uc_c, 0, skip1)
            c2_c = _long_conv(z_c, spec_c, 1, tabs_c)
            yw_c = _dense_attention(qw_c, kw_c, vw_c, sink_rows, batch, nc)
            yg_c = _dense_attention(qg_c, kg_c, vg_c, no_sink, batch, nc)
            xc = _outproj(xc, mod_c, uc_c, c2_c, z_c, skip2, gate_c, hf_c, hb_c, yw_c, yg_c, w_out4, batch * nc)

        xl = _ffn(xl, mod_l, g2, *f2, rows_per_group=n, mod_base=6)
        if not last:
            xc = _ffn(xc, mod_c, g2, *f2, rows_per_group=batch * nc, mod_base=6)

    return _rmsnorm(xl, final_g.reshape(1, d)).reshape(batch, n, d)
```

```python
import functools
import math

import numpy as np
import jax
import jax.numpy as jnp
from jax import lax
from jax.experimental import pallas as pl
from jax.experimental.pallas import tpu as pltpu

F32 = jnp.float32
BF16 = jnp.bfloat16

HEAD_DIM = 64
GRID_W = 64
N_MOD = 9
WINDOW = 128
Q_BLOCK = 128
HY_ORDER = 2
HY_BANDS = 16
HY_MIN_DECAY = math.log(1e-2) / 1.5
HY_MAX_DECAY = math.log(1e-2) / 0.3
LRU_BLOCKS = 4
LRU_C = 8.0
ROPE_BASE = 10000.0
EPS = 1e-6
NEG_INF = -1e30
LOG2E = math.log2(math.e)
Q_SCALE = HEAD_DIM ** -0.5 * LOG2E

LANES = 128
SUBLANES = 8
VMEM_LIMIT = 56 * 1024 * 1024

TOKEN_TILE = 512
FF_CHUNK = 256
CONV_TILE = 256
SCAN_TILE = 256
FLASH_TQ = 512
FLASH_TK = 512
WIN_TQ = 512
FFT_COL_TILE = 1024


def _params(sem, vmem=None):
    return pltpu.CompilerParams(dimension_semantics=sem, vmem_limit_bytes=vmem)


def _split_bf16(x):
    hi = x.astype(BF16)
    lo = (x - hi.astype(F32)).astype(BF16)
    return hi, lo


def _dot(a, b):
    return jnp.dot(a, b, preferred_element_type=F32)


def _dot_nt(a, b):
    return lax.dot_general(a, b, (((1,), (1,)), ((), ())), preferred_element_type=F32)


def _dot3_const(ch, cl, x):
    xh, xl = _split_bf16(x)
    return _dot(ch, xh) + (_dot(cl, xh) + _dot(ch, xl))


def _mod_kernel(c_ref, w_ref, b_ref, o_ref):
    c = c_ref[...]
    a = c * jax.nn.sigmoid(c)
    ah, al = _split_bf16(a)
    wh, wl = _split_bf16(w_ref[0])
    o_ref[0] = _dot(ah, wh) + (_dot(al, wh) + _dot(ah, wl)) + b_ref[0]


def _modulation(cond, w_mod, b_mod):
    depth, d, nd = w_mod.shape
    tn = 1152
    assert nd % tn == 0
    return pl.pallas_call(
        _mod_kernel,
        grid=(depth, nd // tn),
        in_specs=[
            pl.BlockSpec((SUBLANES, d), lambda l, j: (0, 0)),
            pl.BlockSpec((1, d, tn), lambda l, j: (l, 0, j)),
            pl.BlockSpec((1, 1, tn), lambda l, j: (l, 0, j)),
        ],
        out_specs=pl.BlockSpec((1, SUBLANES, tn), lambda l, j: (l, 0, j)),
        out_shape=jax.ShapeDtypeStruct((depth, SUBLANES, nd), F32),
        compiler_params=_params(("parallel", "parallel")),
        name="modulation",
    )(cond, w_mod, b_mod.reshape(depth, 1, nd))


def _adaln(x, g, shift, scale):
    ms = jnp.mean(x * x, axis=-1, keepdims=True)
    y = x * lax.rsqrt(ms + EPS) * g
    return y * (1.0 + scale) + shift


def _group_map(rows_per_group, tile):
    tiles = rows_per_group // tile
    return lambda i: (i // tiles, 0, 0)


def _ffn_kernel(x_ref, mod_ref, g_ref, w1_ref, w3_ref, w2_ref, o_ref, xn_ref, acc_ref, *, mod_base, n_chunks):
    x = x_ref[...]
    shift = mod_ref[0, mod_base:mod_base + 1, :]
    scale = mod_ref[0, mod_base + 1:mod_base + 2, :]
    gate = mod_ref[0, mod_base + 2:mod_base + 3, :]
    xn_ref[...] = _adaln(x, g_ref[...], shift, scale).astype(BF16)
    acc_ref[...] = jnp.zeros_like(acc_ref)

    def body(c, carry):
        xn = xn_ref[...]
        h = _dot(xn, w1_ref[c])
        u = _dot(xn, w3_ref[c])
        a = (h * jax.nn.sigmoid(h) * u).astype(BF16)
        acc_ref[...] += _dot(a, w2_ref[c])
        return carry

    lax.fori_loop(0, n_chunks, body, 0, unroll=True)
    o_ref[...] = x + 0.5 * gate * acc_ref[...]


def _ffn(x, mod, g, w1c, w3c, w2c, rows_per_group, mod_base):
    rows, d = x.shape
    tm = TOKEN_TILE
    assert rows % tm == 0 and rows_per_group % tm == 0
    n_chunks, _, ch = w1c.shape
    const3 = lambda i: (0, 0, 0)
    return pl.pallas_call(
        functools.partial(_ffn_kernel, mod_base=mod_base, n_chunks=n_chunks),
        grid=(rows // tm,),
        in_specs=[
            pl.BlockSpec((tm, d), lambda i: (i, 0)),
            pl.BlockSpec((1, N_MOD, d), _group_map(rows_per_group, tm)),
            pl.BlockSpec((1, d), lambda i: (0, 0)),
            pl.BlockSpec((n_chunks, d, ch), const3, pipeline_mode=pl.Buffered(1)),
            pl.BlockSpec((n_chunks, d, ch), const3, pipeline_mode=pl.Buffered(1)),
            pl.BlockSpec((n_chunks, ch, d), const3, pipeline_mode=pl.Buffered(1)),
        ],
        out_specs=pl.BlockSpec((tm, d), lambda i: (i, 0)),
        out_shape=jax.ShapeDtypeStruct((rows, d), F32),
        scratch_shapes=[pltpu.VMEM((tm, d), BF16), pltpu.VMEM((tm, d), F32)],
        compiler_params=_params(("parallel",), VMEM_LIMIT),
        name="ffn",
    )(x, mod, g, w1c, w3c, w2c)


def _ffn_weights(w13, w2):
    d, two_ff = w13.shape
    ff = two_ff // 2
    assert ff % FF_CHUNK == 0
    nc = ff // FF_CHUNK
    w1c = w13[:, :ff].astype(BF16).reshape(d, nc, FF_CHUNK).transpose(1, 0, 2)
    w3c = w13[:, ff:].astype(BF16).reshape(d, nc, FF_CHUNK).transpose(1, 0, 2)
    w2c = w2.astype(BF16).reshape(nc, FF_CHUNK, d)
    return w1c, w3c, w2c


def _swap_halves(z):
    lane = lax.broadcasted_iota(jnp.int32, z.shape, 1)
    first = (lane % HEAD_DIM) < (HEAD_DIM // 2)
    return jnp.where(first, pltpu.roll(z, LANES - HEAD_DIM // 2, 1), pltpu.roll(z, HEAD_DIM // 2, 1))


def _mixin_kernel(*refs, use_rope, d_hy3, d_lru, d_q, d_kv):
    if use_rope:
        (x_ref, mod_ref, g_ref, w_ref, gq_ref, gk_ref, hs_ref, cos_ref, sin_ref,
         uin_ref, gate_ref, qw_ref, kw_ref, vw_ref, qg_ref, kg_ref, vg_ref) = refs
    else:
        (x_ref, mod_ref, g_ref, w_ref, gq_ref, gk_ref, hs_ref,
         uin_ref, gate_ref, qw_ref, kw_ref, vw_ref, qg_ref, kg_ref, vg_ref) = refs
        cos_ref = sin_ref = None
    x = x_ref[...]
    xn = _adaln(x, g_ref[...], mod_ref[0, 3:4, :], mod_ref[0, 4:5, :]).astype(BF16)

    def proj(lo, width):
        return _dot(xn, w_ref[:, lo:lo + width])

    def rope(z):
        if not use_rope:
            return z
        return z * cos_ref[...] + _swap_halves(z) * sin_ref[...]

    def head_norm(z, gain):
        z2 = z * z
        zh, zl = _split_bf16(z2)
        ms = _dot(zh, hs_ref[...]) + _dot(zl, hs_ref[...])
        return z * lax.rsqrt(ms + EPS) * gain

    def store_values(v_ref, v):
        assert d_kv == LANES
        low = lax.broadcasted_iota(jnp.int32, v.shape, 1) < HEAD_DIM
        v_ref[:, 0:LANES] = jnp.where(low, v, 1.0).astype(BF16)
        v_ref[:, LANES:2 * LANES] = jnp.where(low, pltpu.roll(v, HEAD_DIM, 1), 1.0).astype(BF16)

    off = 0
    uin_ref[...] = proj(off, d_hy3 + d_lru)
    off += d_hy3 + d_lru
    gate_ref[...] = proj(off, d_lru)
    off += d_lru
    scale = Q_SCALE
    z = proj(off, d_q)
    for c in range(d_q // LANES):
        qw_ref[:, c * LANES:(c + 1) * LANES] = (rope(z[:, c * LANES:(c + 1) * LANES]) * scale).astype(BF16)
    off += d_q
    z = proj(off, 2 * d_kv)
    kw_ref[...] = rope(z[:, :d_kv]).astype(BF16)
    store_values(vw_ref, z[:, d_kv:])
    off += 2 * d_kv
    z = proj(off, d_q)
    for c in range(d_q // LANES):
        zn = head_norm(z[:, c * LANES:(c + 1) * LANES], gq_ref[...])
        qg_ref[:, c * LANES:(c + 1) * LANES] = (rope(zn) * scale).astype(BF16)
    off += d_q
    z = proj(off, 2 * d_kv)
    kg_ref[...] = rope(head_norm(z[:, :d_kv], gk_ref[...])).astype(BF16)
    store_values(vg_ref, z[:, d_kv:])


def _mixin(x, mod, g, w_in, gq, gk, hsum, rope_tabs, rows_per_group, seq_len, dims):
    rows, d = x.shape
    d_hy3, d_lru, d_q, d_kv = dims
    tm = TOKEN_TILE
    assert rows % tm == 0 and rows_per_group % tm == 0
    use_rope = rope_tabs is not None
    d_in = w_in.shape[1]
    row = lambda i: (i, 0)
    const2 = lambda i: (0, 0)
    in_specs = [
        pl.BlockSpec((tm, d), row),
        pl.BlockSpec((1, N_MOD, d), _group_map(rows_per_group, tm)),
        pl.BlockSpec((1, d), const2),
        pl.BlockSpec((d, d_in), const2, pipeline_mode=pl.Buffered(1)),
        pl.BlockSpec((1, LANES), const2),
        pl.BlockSpec((1, LANES), const2),
        pl.BlockSpec((LANES, LANES), const2),
    ]
    args = [x, mod, g, w_in, gq, gk, hsum]
    if use_rope:
        assert seq_len % tm == 0
        tiles_per_seq = seq_len // tm
        tab = lambda i: (i % tiles_per_seq, 0)
        in_specs += [pl.BlockSpec((tm, LANES), tab), pl.BlockSpec((tm, LANES), tab)]
        args += list(rope_tabs)
    widths = (d_hy3 + d_lru, d_lru, d_q, d_kv, 2 * d_kv, d_q, d_kv, 2 * d_kv)
    dtypes = (F32, F32, BF16, BF16, BF16, BF16, BF16, BF16)
    return pl.pallas_call(
        functools.partial(_mixin_kernel, use_rope=use_rope, d_hy3=d_hy3, d_lru=d_lru, d_q=d_q, d_kv=d_kv),
        grid=(rows // tm,),
        in_specs=in_specs,
        out_specs=[pl.BlockSpec((tm, w), row) for w in widths],
        out_shape=[jax.ShapeDtypeStruct((rows, w), dt) for w, dt in zip(widths, dtypes)],
        compiler_params=_params(("parallel",), VMEM_LIMIT),
        name="mixin_rope" if use_rope else "mixin",
    )(*args)


def _shortconv_kernel(cur_ref, prev_ref, next_ref, w_ref, b_ref, o_ref, ext_ref, *, tiles_per_seq):
    ts = cur_ref.shape[0]
    j = pl.program_id(0) % tiles_per_seq
    keep_prev = (j > 0).astype(F32)
    keep_next = (j < tiles_per_seq - 1).astype(F32)
    ext_ref[0:SUBLANES, :] = prev_ref[...] * keep_prev
    ext_ref[SUBLANES:SUBLANES + ts, :] = cur_ref[...]
    ext_ref[SUBLANES + ts:, :] = next_ref[...] * keep_next
    acc = b_ref[...] + w_ref[2:3, :] * cur_ref[...]
    for k in (0, 1, 3):
        acc = acc + w_ref[k:k + 1, :] * ext_ref[SUBLANES - 2 + k:SUBLANES - 2 + k + ts, :]
    o_ref[...] = acc


def _shortconv(u, w4, bias, seq_len):
    rows, ch = u.shape
    ts = min(CONV_TILE, seq_len)
    assert seq_len % ts == 0 and rows % ts == 0 and ts % SUBLANES == 0
    tiles_per_seq = seq_len // ts
    sub = ts // SUBLANES
    last = rows // SUBLANES - 1
    return pl.pallas_call(
        functools.partial(_shortconv_kernel, tiles_per_seq=tiles_per_seq),
        grid=(rows // ts,),
        in_specs=[
            pl.BlockSpec((ts, ch), lambda i: (i, 0)),
            pl.BlockSpec((SUBLANES, ch), lambda i: (jnp.maximum(i * sub - 1, 0), 0)),
            pl.BlockSpec((SUBLANES, ch), lambda i: (jnp.minimum((i + 1) * sub, last), 0)),
            pl.BlockSpec((4, ch), lambda i: (0, 0)),
            pl.BlockSpec((1, ch), lambda i: (0, 0)),
        ],
        out_specs=pl.BlockSpec((ts, ch), lambda i: (i, 0)),
        out_shape=jax.ShapeDtypeStruct((rows, ch), F32),
        scratch_shapes=[pltpu.VMEM((ts + 2 * SUBLANES, ch), F32)],
        compiler_params=_params(("parallel",)),
        name="shortconv",
    )(u, u, u, w4, bias)


def _fft_split(n):
    big = 2 * n
    n2 = LANES if big >= 16 * LANES else LANES // 4
    assert big % n2 == 0 and (big // n2) % 16 == 0
    return big // n2, n2


def _np_hi_lo(m):
    m32 = jnp.asarray(np.asarray(m, dtype=np.float32))
    hi = m32.astype(BF16)
    lo = (m32 - hi.astype(F32)).astype(BF16)
    return hi, lo


@functools.lru_cache(maxsize=None)
def _fft_tables(n):
    n1, n2 = _fft_split(n)
    big = n1 * n2
    h1 = n1 // 2
    f1 = np.arange(n1)[:, None].astype(np.float64)
    s1 = np.arange(h1)[None, :].astype(np.float64)
    th = 2.0 * np.pi * f1 * s1 / n1
    c, s = np.cos(th), np.sin(th)
    w1 = np.block([[c, s], [-s, c]])
    w3 = np.block([[c.T, -s.T], [s.T, c.T]]) / big
    s1f = np.arange(n1)[None, :].astype(np.float64)
    thf = 2.0 * np.pi * f1 * s1f / n1
    w1f = np.concatenate([np.cos(thf), -np.sin(thf)], axis=0)
    f2 = np.arange(n2)[:, None].astype(np.float64)
    s2 = np.arange(n2)[None, :].astype(np.float64)
    th2 = 2.0 * np.pi * f2 * s2 / n2
    eye = np.eye(LANES // n2)
    c2, sn2 = np.kron(eye, np.cos(th2)), np.kron(eye, np.sin(th2))
    m2 = np.block([[c2, -sn2], [sn2, c2]])
    ph = 2.0 * np.pi * np.arange(n1)[:, None] * np.arange(n2)[None, :] / big
    tw = np.stack([np.cos(ph), -np.sin(ph)], axis=0)
    tw = np.tile(tw, (1, 1, LANES // n2)).astype(np.float32)
    return dict(n1=n1, n2=n2, w1=w1, w3=w3, w1f=w1f, m2=m2, m2t=m2.T, tw=tw)


def _dot3_rconst(x, ch, cl):
    xh, xl = _split_bf16(x)
    return _dot(xh, ch) + (_dot(xl, ch) + _dot(xh, cl))


def _fftconv_kernel(*refs, fwd_only, n1, groups):
    if fwd_only:
        v_ref, inv_ref, tw_ref, w1h_ref, w1l_ref, mh_ref, ml_ref, o_ref = refs
    else:
        (v_ref, sr_ref, si_ref, tw_ref, w1h_ref, w1l_ref, mh_ref, ml_ref, mth_ref, mtl_ref,
         w3h_ref, w3l_ref, o_ref) = refs

    def stack(x):
        return jnp.concatenate([x[:, g * LANES:(g + 1) * LANES] for g in range(groups)], axis=0)

    def unstack(x):
        return jnp.concatenate([x[g * n1:(g + 1) * n1] for g in range(groups)], axis=1)

    a = _dot3_const(w1h_ref[...], w1l_ref[...], v_ref[...])
    twr = jnp.concatenate([tw_ref[0]] * groups, axis=0)
    twi = jnp.concatenate([tw_ref[1]] * groups, axis=0)
    ar, ai = stack(a[:n1]), stack(a[n1:])
    b = jnp.concatenate([ar * twr - ai * twi, ar * twi + ai * twr], axis=1)
    x = _dot3_rconst(b, mh_ref[...], ml_ref[...])
    xr, xi = x[:, :LANES], x[:, LANES:]
    if fwd_only:
        o_ref[0:n1, :] = unstack(xr) * inv_ref[...]
        o_ref[n1:2 * n1, :] = unstack(xi) * inv_ref[...]
        return
    kr, ki = stack(sr_ref[...]), stack(si_ref[...])
    y = jnp.concatenate([xr * kr - xi * ki, xr * ki + xi * kr], axis=1)
    c = _dot3_rconst(y, mth_ref[...], mtl_ref[...])
    cr, ci = c[:, :LANES], c[:, LANES:]
    d = jnp.concatenate([unstack(cr * twr + ci * twi), unstack(ci * twr - cr * twi)], axis=0)
    o_ref[...] = _dot3_const(w3h_ref[...], w3l_ref[...], d)


def _fftconv(v2, spec, order, tabs, fwd_only):
    n1 = tabs["n1"]
    cols = v2.shape[1]
    tc = min(FFT_COL_TILE, cols)
    assert cols % tc == 0 and v2.shape[0] == n1
    nblk = cols // tc
    const2 = lambda j: (0, 0)
    const3 = lambda j: (0, 0, 0)
    w1h, w1l = _np_hi_lo(tabs["w1f"] if fwd_only else tabs["w1"])
    mh, ml = _np_hi_lo(tabs["m2"])
    in_specs = [pl.BlockSpec((n1, tc), lambda j: (0, j))]
    args = [v2]
    if fwd_only:
        in_specs.append(pl.BlockSpec((1, tc), lambda j: (0, j)))
        args.append(spec)
    else:
        in_specs += [pl.BlockSpec((n1, tc), lambda j: (0, order * nblk + j)),
                     pl.BlockSpec((n1, tc), lambda j: (1, order * nblk + j))]
        args += [spec, spec]
    in_specs += [pl.BlockSpec((2, n1, LANES), const3), pl.BlockSpec(w1h.shape, const2), pl.BlockSpec(w1h.shape, const2),
                 pl.BlockSpec(mh.shape, const2), pl.BlockSpec(mh.shape, const2)]
    args += [jnp.asarray(tabs["tw"]), w1h, w1l, mh, ml]
    if not fwd_only:
        mth, mtl = _np_hi_lo(tabs["m2t"])
        w3h, w3l = _np_hi_lo(tabs["w3"])
        in_specs += [pl.BlockSpec(mh.shape, const2), pl.BlockSpec(mh.shape, const2),
                     pl.BlockSpec(w3h.shape, const2), pl.BlockSpec(w3h.shape, const2)]
        args += [mth, mtl, w3h, w3l]
    out_rows = 2 * n1 if fwd_only else n1
    return pl.pallas_call(
        functools.partial(_fftconv_kernel, fwd_only=fwd_only, n1=n1, groups=tc // LANES),
        grid=(nblk,),
        in_specs=in_specs,
        out_specs=pl.BlockSpec((out_rows, tc), lambda j: (0, j)),
        out_shape=jax.ShapeDtypeStruct((out_rows, cols), F32),
        compiler_params=_params(("parallel",), VMEM_LIMIT),
        name="fft_filter" if fwd_only else "fft_conv",
    )(*args)


def _to_fft_layout(v, n1, n2):
    ch = v.shape[1]
    return v.reshape(n1, n2, ch).transpose(0, 2, 1).reshape(n1, ch * n2)


def _from_fft_layout(y, n2, ch):
    rows = y.shape[0]
    return y.reshape(rows, ch, n2).transpose(0, 2, 1).reshape(rows * n2, ch)


def _filter_spectrum(k, norm, tabs):
    inv = jnp.repeat(1.0 / norm, tabs["n2"], axis=1)
    return _fftconv(_to_fft_layout(k, tabs["n1"], tabs["n2"]), inv, 0, tabs, True)


def _long_conv(v, spec, order, tabs):
    n1, n2 = tabs["n1"], tabs["n2"]
    y = _fftconv(_to_fft_layout(v, n1, n2), spec, order, tabs, False)
    return _from_fft_layout(y, n2, v.shape[1])


FILTER_FEATS = 64


@functools.lru_cache(maxsize=None)
def _filter_features(n):
    r = np.arange(2 * n)
    pos = np.where(r < n, r, np.where(r == n, 0, 2 * n - r))
    t = np.linspace(0.0, 1.0, n, dtype=np.float32)[pos]
    w = (np.float32(2.0 * math.pi) * pos.astype(np.float32)) / np.float32(n)
    f = np.linspace(1e-4, HY_BANDS - 1, HY_BANDS, dtype=np.float32)
    fw = (f[None, :] * w[:, None]).astype(np.float32).astype(np.float64)
    z = np.zeros((2 * n, FILTER_FEATS), np.float32)
    z[:, 0] = t
    z[:, 1:1 + HY_BANDS] = np.cos(fw)
    z[:, 1 + HY_BANDS:1 + 2 * HY_BANDS] = -np.sin(fw)
    return z


def _dot3(a, b):
    ah, al = _split_bf16(a)
    bh, bl = _split_bf16(b)
    return _dot(ah, bh) + (_dot(al, bh) + _dot(ah, bl))


def _filter_kernel(z_ref, w1_ref, b1_ref, fr_ref, w2_ref, b2_ref, w3_ref, dl_ref, k_ref, sum_ref, *, n):
    i = pl.program_id(0)
    tile = z_ref.shape[0]
    z = z_ref[...]
    h = jnp.sin(fr_ref[0:1, :] * (_dot3(z, w1_ref[...]) + b1_ref[...]))
    h = jnp.sin(fr_ref[1:2, :] * (_dot3(h, w2_ref[...]) + b2_ref[...]))
    k = _dot3(h, w3_ref[0]) * jnp.exp(-z[:, 0:1] * dl_ref[...])

    @pl.when(i == 0)
    def _():
        sum_ref[...] = jnp.zeros_like(sum_ref)

    sum_ref[...] += jnp.sum(jnp.abs(k), axis=0, keepdims=True)
    row = i * tile + lax.broadcasted_iota(jnp.int32, (tile, 1), 0)
    k_ref[...] = jnp.where(row == n, 0.0, k)


def _hyena_filter(n, w1, b1, freq, w2, b2, w3, d_hy):
    z = jnp.asarray(_filter_features(n))
    hid = w1.shape[1]
    cols = HY_ORDER * d_hy
    tile = min(TOKEN_TILE, n)
    assert n % tile == 0
    half = n // tile
    w1p = jnp.zeros((FILTER_FEATS, hid), F32).at[:w1.shape[0]].set(w1)
    w3d = w3.reshape(hid, HY_ORDER, 2, d_hy).transpose(2, 0, 1, 3).reshape(2, hid, cols)
    deltas = jnp.abs(jnp.linspace(HY_MIN_DECAY, HY_MAX_DECAY, d_hy, dtype=F32))
    dl = jnp.tile(deltas, HY_ORDER).reshape(1, cols)
    const2 = lambda i: (0, 0)
    k, sums = pl.pallas_call(
        functools.partial(_filter_kernel, n=n),
        grid=(2 * half,),
        in_specs=[
            pl.BlockSpec((tile, FILTER_FEATS), lambda i: (i, 0)),
            pl.BlockSpec((FILTER_FEATS, hid), const2),
            pl.BlockSpec((1, hid), const2),
            pl.BlockSpec((2, hid), const2),
            pl.BlockSpec((hid, hid), const2),
            pl.BlockSpec((1, hid), const2),
            pl.BlockSpec((1, hid, cols), lambda i: (i // half, 0, 0)),
            pl.BlockSpec((1, cols), const2),
        ],
        out_specs=[pl.BlockSpec((tile, cols), lambda i: (i, 0)), pl.BlockSpec((1, cols), const2)],
        out_shape=[jax.ShapeDtypeStruct((2 * n, cols), F32), jax.ShapeDtypeStruct((1, cols), F32)],
        compiler_params=_params(("arbitrary",)),
        name="hyena_filter",
    )(z, w1p, b1.reshape(1, hid), freq, w2, b2.reshape(1, hid), w3d, dl)
    return k, sums + EPS


def _gate_kernel(a_ref, b_ref, c_ref, s_ref, o_ref):
    o_ref[...] = a_ref[...] * (b_ref[...] + c_ref[...] * s_ref[...])


def _hyena_gate(uc, a_col, conv, c_src, c_col, skip):
    rows, ch = conv.shape
    tm = min(TOKEN_TILE, rows)
    assert rows % tm == 0
    return pl.pallas_call(
        _gate_kernel,
        grid=(rows // tm,),
        in_specs=[
            pl.BlockSpec((tm, ch), lambda i: (i, a_col)),
            pl.BlockSpec((tm, ch), lambda i: (i, 0)),
            pl.BlockSpec((tm, ch), lambda i: (i, c_col)),
            pl.BlockSpec((1, ch), lambda i: (0, 0)),
        ],
        out_specs=pl.BlockSpec((tm, ch), lambda i: (i, 0)),
        out_shape=jax.ShapeDtypeStruct((rows, ch), F32),
        compiler_params=_params(("parallel",)),
        name="hyena_gate",
    )(uc, conv, c_src, skip)


def _shift_rows(x, s, fill, reverse):
    t = x.shape[0]
    row = lax.broadcasted_iota(jnp.int32, x.shape, 0) % SUBLANES
    if reverse:
        return jnp.where(row >= SUBLANES - s, fill, pltpu.roll(x, t - s, 0))
    return jnp.where(row < s, fill, pltpu.roll(x, s, 0))


def _lru_kernel(uf_ref, ub_ref, wa_ref, wx_ref, p_ref, h0_ref, hf_ref, hb_ref, hl_ref, carry_ref):
    j = pl.program_id(1)
    nt = pl.num_programs(1)
    tl = uf_ref.shape[0]

    @pl.when(j == 0)
    def _():
        carry_ref[...] = h0_ref[0]

    for d, (u_ref, o_ref) in enumerate(((uf_ref, hf_ref), (ub_ref, hb_ref))):
        reverse = d == 1
        u = u_ref[...]
        ub16 = u.astype(BF16)
        r = jax.nn.sigmoid(_dot(ub16, wa_ref[d]) + p_ref[d, 0:1, :])
        gi = jax.nn.sigmoid(_dot(ub16, wx_ref[d]) + p_ref[d, 1:2, :])
        nlam = -p_ref[d, 2:3, :]
        softplus = jnp.maximum(nlam, 0.0) + jnp.log1p(jnp.exp(-jnp.abs(nlam)))
        log_a = -LRU_C * r * softplus
        a = jnp.exp(log_a)
        b = jnp.sqrt(-jnp.tanh(log_a) * (a * a + 1.0)) * gi * u
        s = 1
        while s < SUBLANES:
            a_s = _shift_rows(a, s, 1.0, reverse)
            b_s = _shift_rows(b, s, 0.0, reverse)
            b = a * b_s + b
            a = a * a_s
            s *= 2
        carry = carry_ref[d:d + 1, :]
        groups = range(tl // SUBLANES)
        for g in (reversed(groups) if reverse else groups):
            rows = slice(g * SUBLANES, (g + 1) * SUBLANES)
            hg = b[rows] + a[rows] * carry
            o_ref[rows, :] = hg
            carry = hg[0:1, :] if reverse else hg[SUBLANES - 1:SUBLANES, :]
        carry_ref[d:d + 1, :] = carry

    @pl.when(j == nt - 1)
    def _():
        hl_ref[0] = carry_ref[...]


def _lru(uc, col, wa, wx, p, h0, batch, seq_len):
    ch = wa.shape[-1]
    tl = min(SCAN_TILE, seq_len)
    assert seq_len % tl == 0
    nt = seq_len // tl
    rows = batch * seq_len
    fwd = lambda b, j: (b * nt + j, col)
    bwd = lambda b, j: (b * nt + nt - 1 - j, col)
    fwd_o = lambda b, j: (b * nt + j, 0)
    bwd_o = lambda b, j: (b * nt + nt - 1 - j, 0)
    const3 = lambda b, j: (0, 0, 0)
    return pl.pallas_call(
        _lru_kernel,
        grid=(batch, nt),
        in_specs=[
            pl.BlockSpec((tl, ch), fwd),
            pl.BlockSpec((tl, ch), bwd),
            pl.BlockSpec((2, ch, ch), const3),
            pl.BlockSpec((2, ch, ch), const3),
            pl.BlockSpec((2, 3, ch), const3),
            pl.BlockSpec((1, 2, ch), lambda b, j: (b, 0, 0)),
        ],
        out_specs=[
            pl.BlockSpec((tl, ch), fwd_o),
            pl.BlockSpec((tl, ch), bwd_o),
            pl.BlockSpec((1, 2, ch), lambda b, j: (b, 0, 0)),
        ],
        out_shape=[
            jax.ShapeDtypeStruct((rows, ch), F32),
            jax.ShapeDtypeStruct((rows, ch), F32),
            jax.ShapeDtypeStruct((batch, 2, ch), F32),
        ],
        scratch_shapes=[pltpu.VMEM((2, ch), F32)],
        compiler_params=_params(("parallel", "arbitrary")),
        name="rglru",
    )(uc, uc, wa, wx, p, h0)


def _block_diag(w):
    two, nb, d, _ = w.shape
    eye = jnp.eye(nb, dtype=w.dtype)
    return jnp.einsum('xnde,nm->xndme', w, eye).reshape(two, nb * d, nb * d)


def _expand_q(q_ref, qe_ref):
    tq = q_ref.shape[0]
    half = LANES // 2
    low = lax.broadcasted_iota(jnp.int32, (tq, LANES), 1) < half
    for kv in range(2):
        qf = q_ref[:, kv * LANES:(kv + 1) * LANES].astype(F32)
        qr = pltpu.roll(qf, half, 1)
        if kv == 0:
            g0, g1 = jnp.where(low, qf, 0.0), jnp.where(low, qr, 0.0)
        else:
            g0, g1 = jnp.where(low, 0.0, qr), jnp.where(low, 0.0, qf)
        qe_ref[kv, 0:tq, :] = g0.astype(BF16)
        qe_ref[kv, tq:2 * tq, :] = g1.astype(BF16)


def _row_max(s):
    m = s[:, 0:LANES]
    for c in range(1, s.shape[1] // LANES):
        m = jnp.maximum(m, s[:, c * LANES:(c + 1) * LANES])
    return jnp.max(m, axis=-1, keepdims=True)


def _exp2_bf16(s, m):
    return jnp.concatenate(
        [jnp.exp2(s[:, c * LANES:(c + 1) * LANES] - m).astype(BF16) for c in range(s.shape[1] // LANES)], axis=1)


def _finish_heads(acc, tq):
    half = LANES // 2
    o = acc / pltpu.roll(acc, half, 1)
    low = lax.broadcasted_iota(jnp.int32, (tq, LANES), 1) < half
    return jnp.where(low, o[0:tq], pltpu.roll(o[tq:2 * tq], half, 1))


def _win_kernel(q_ref, kp_ref, kc_ref, kn_ref, vp_ref, vc_ref, vn_ref, kx_ref, vx_ref, sink_ref, o_ref, qe_ref,
                *, seq_len):
    i = pl.program_id(1)
    tq = q_ref.shape[0]
    halo = kp_ref.shape[0]
    _expand_q(q_ref, qe_ref)
    kk = jnp.concatenate([kp_ref[...], kc_ref[...], kn_ref[...]], axis=0)
    vv = jnp.concatenate([vp_ref[...], vc_ref[...], vn_ref[...]], axis=0)
    n_loc = tq + 2 * halo
    row = lax.broadcasted_iota(jnp.int32, (tq, n_loc), 0)
    col = lax.broadcasted_iota(jnp.int32, (tq, n_loc), 1)
    kpos = i * tq - halo + col
    valid = (jnp.abs(col - halo - row) <= WINDOW) & (kpos >= 0) & (kpos < seq_len)
    valid = jnp.concatenate([valid, valid], axis=0)
    high = lax.broadcasted_iota(jnp.int32, (2 * tq, LANES), 1) >= LANES // 2
    for kv in range(2):
        vsl = slice(kv * LANES, (kv + 1) * LANES)
        s_loc = jnp.where(valid, _dot_nt(qe_ref[kv], kk), NEG_INF)
        s_ctx = _dot_nt(qe_ref[kv], kx_ref[...])
        sink = jnp.concatenate([jnp.broadcast_to(sink_ref[2 * kv + g:2 * kv + g + 1, :], (tq, LANES))
                                for g in range(2)], axis=0)
        m = jnp.maximum(jnp.maximum(_row_max(s_loc), _row_max(s_ctx)), sink)
        acc = _dot(_exp2_bf16(s_loc, m), vv[:, vsl]) + _dot(_exp2_bf16(s_ctx, m), vx_ref[:, vsl])
        acc = acc + jnp.where(high, jnp.exp2(sink - m), 0.0)
        o_ref[:, vsl] = _finish_heads(acc, tq).astype(o_ref.dtype)


def _window_attention(q, k, v_ext, kx, vx_ext, sink_rows, batch, seq_len, ctx_len):
    tq = min(WIN_TQ, seq_len)
    halo = Q_BLOCK
    assert seq_len % tq == 0 and tq % halo == 0 and WINDOW <= halo
    nb = seq_len // tq
    per = tq // halo
    last = seq_len // halo - 1
    dq, dk, dv = q.shape[1], k.shape[1], v_ext.shape[1]
    assert dk == LANES and dq == 2 * LANES and dv == 2 * LANES
    cur = lambda b, i: (b * nb + i, 0)
    prv = lambda b, i: (b * (last + 1) + jnp.maximum(i * per - 1, 0), 0)
    nxt = lambda b, i: (b * (last + 1) + jnp.minimum((i + 1) * per, last), 0)
    cx = lambda b, i: (b, 0)
    return pl.pallas_call(
        functools.partial(_win_kernel, seq_len=seq_len),
        grid=(batch, nb),
        in_specs=[
            pl.BlockSpec((tq, dq), cur),
            pl.BlockSpec((halo, dk), prv), pl.BlockSpec((tq, dk), cur), pl.BlockSpec((halo, dk), nxt),
            pl.BlockSpec((halo, dv), prv), pl.BlockSpec((tq, dv), cur), pl.BlockSpec((halo, dv), nxt),
            pl.BlockSpec((ctx_len, dk), cx), pl.BlockSpec((ctx_len, dv), cx),
            pl.BlockSpec(sink_rows.shape, lambda b, i: (0, 0)),
        ],
        out_specs=pl.BlockSpec((tq, dq), cur),
        out_shape=jax.ShapeDtypeStruct((batch * seq_len, dq), BF16),
        scratch_shapes=[pltpu.VMEM((2, 2 * tq, LANES), BF16)],
        compiler_params=_params(("parallel", "parallel"), VMEM_LIMIT),
        name="window_attention",
    )(q, k, k, k, v_ext, v_ext, v_ext, kx, vx_ext, sink_rows)


def _dense_kernel(q_ref, k_ref, v_ref, sink_ref, o_ref, qe_ref):
    tq = q_ref.shape[0]
    _expand_q(q_ref, qe_ref)
    high = lax.broadcasted_iota(jnp.int32, (2 * tq, LANES), 1) >= LANES // 2
    for kv in range(2):
        vsl = slice(kv * LANES, (kv + 1) * LANES)
        s = _dot_nt(qe_ref[kv], k_ref[...])
        sink = jnp.concatenate([jnp.broadcast_to(sink_ref[2 * kv + g:2 * kv + g + 1, :], (tq, LANES))
                                for g in range(2)], axis=0)
        m = jnp.maximum(_row_max(s), sink)
        acc = _dot(_exp2_bf16(s, m), v_ref[:, vsl]) + jnp.where(high, jnp.exp2(sink - m), 0.0)
        o_ref[:, vsl] = _finish_heads(acc, tq).astype(o_ref.dtype)


def _dense_attention(q, k, v_ext, sink_rows, batch, seq_len):
    dq, dk, dv = q.shape[1], k.shape[1], v_ext.shape[1]
    assert dk == LANES and dq == 2 * LANES and dv == 2 * LANES
    blk = lambda b: (b, 0)
    return pl.pallas_call(
        _dense_kernel,
        grid=(batch,),
        in_specs=[pl.BlockSpec((seq_len, dq), blk), pl.BlockSpec((seq_len, dk), blk), pl.BlockSpec((seq_len, dv), blk),
                  pl.BlockSpec(sink_rows.shape, lambda b: (0, 0))],
        out_specs=pl.BlockSpec((seq_len, dq), blk),
        out_shape=jax.ShapeDtypeStruct((batch * seq_len, dq), BF16),
        scratch_shapes=[pltpu.VMEM((2, 2 * seq_len, LANES), BF16)],
        compiler_params=_params(("parallel",)),
        name="dense_attention",
    )(q, k, v_ext, sink_rows)


def _flash_kernel(q_ref, k_ref, v_ref, kx_ref, vx_ref, o_ref, qe_ref, m_ref, acc_ref, s_ref, mx_ref, *, tk):
    tq = q_ref.shape[0]
    nb = k_ref.shape[0] // tk
    _expand_q(q_ref, qe_ref)
    m_ref[...] = jnp.full_like(m_ref, NEG_INF)
    acc_ref[...] = jnp.zeros_like(acc_ref)

    def update(kv, s, row_max, vblk):
        m_old = m_ref[kv]
        m_new = jnp.maximum(m_old, row_max)
        alpha = jnp.exp2(m_old - m_new)
        acc_ref[kv] = alpha * acc_ref[kv] + _dot(_exp2_bf16(s, m_new), vblk)
        m_ref[kv] = m_new

    for kv in range(2):
        s = _dot_nt(qe_ref[kv], kx_ref[...])
        update(kv, s, _row_max(s), vx_ref[:, kv * LANES:(kv + 1) * LANES])

    def scores(j, slot):
        kblk = k_ref[pl.ds(pl.multiple_of(j * tk, tk), tk), :]
        for kv in range(2):
            s = _dot_nt(qe_ref[kv], kblk)
            s_ref[slot, kv] = s
            mx_ref[slot, kv] = jnp.broadcast_to(_row_max(s), mx_ref.shape[2:])

    def consume(j, slot):
        off = pl.multiple_of(j * tk, tk)
        for kv in range(2):
            update(kv, s_ref[slot, kv], mx_ref[slot, kv], v_ref[pl.ds(off, tk), kv * LANES:(kv + 1) * LANES])

    scores(0, 0)

    def body(jj, carry):
        j = 2 * jj
        scores(j + 1, 1)
        consume(j, 0)
        scores(j + 2, 0)
        consume(j + 1, 1)
        return carry

    lax.fori_loop(0, nb // 2 - 1, body, 0)
    scores(nb - 1, 1)
    consume(nb - 2, 0)
    consume(nb - 1, 1)
    for kv in range(2):
        o_ref[:, kv * LANES:(kv + 1) * LANES] = _finish_heads(acc_ref[kv], tq).astype(o_ref.dtype)


def _global_attention(q, k, v_ext, kx, vx_ext, batch, seq_len, ctx_len):
    tq = min(FLASH_TQ, seq_len)
    tk = min(FLASH_TK, seq_len)
    assert seq_len % tq == 0 and seq_len % (2 * tk) == 0
    nq = seq_len // tq
    dq, dk, dv = q.shape[1], k.shape[1], v_ext.shape[1]
    assert dk == LANES and dq == 2 * LANES and dv == 2 * LANES
    qmap = lambda b, i: (b * nq + i, 0)
    whole = lambda b, i: (b, 0)
    return pl.pallas_call(
        functools.partial(_flash_kernel, tk=tk),
        grid=(batch, nq),
        in_specs=[
            pl.BlockSpec((tq, dq), qmap),
            pl.BlockSpec((seq_len, dk), whole), pl.BlockSpec((seq_len, dv), whole),
            pl.BlockSpec((ctx_len, dk), whole), pl.BlockSpec((ctx_len, dv), whole),
        ],
        out_specs=pl.BlockSpec((tq, dq), qmap),
        out_shape=jax.ShapeDtypeStruct((batch * seq_len, dq), BF16),
        scratch_shapes=[
            pltpu.VMEM((2, 2 * tq, LANES), BF16),
            pltpu.VMEM((2, 2 * tq, LANES), F32),
            pltpu.VMEM((2, 2 * tq, LANES), F32),
            pltpu.VMEM((2, 2, 2 * tq, tk), F32),
            pltpu.VMEM((2, 2, 2 * tq, LANES), F32),
        ],
        compiler_params=_params(("parallel", "parallel"), VMEM_LIMIT),
        name="global_attention",
    )(q, k, v_ext, kx, vx_ext)


def _outproj_kernel(x_ref, mod_ref, x2_ref, conv_ref, z_ref, skip_ref, gate_ref, hf_ref, hb_ref,
                    yw_ref, yg_ref, w_ref, o_ref):
    y_hy = x2_ref[...] * (conv_ref[...] + z_ref[...] * skip_ref[...])
    y_lru = jax.nn.gelu(gate_ref[...]) * (hf_ref[...] + hb_ref[...])
    acc = _dot(y_hy.astype(BF16), w_ref[0])
    acc = acc + _dot(y_lru.astype(BF16), w_ref[1])
    acc = acc + _dot(yw_ref[...], w_ref[2])
    acc = acc + _dot(yg_ref[...], w_ref[3])
    o_ref[...] = x_ref[...] + mod_ref[0, 5:6, :] * acc


def _outproj(x, mod, uc, conv2, z, skip2, gate, hf, hb, yw, yg, w_out4, rows_per_group):
    rows, d = x.shape
    tm = TOKEN_TILE
    assert rows % tm == 0 and rows_per_group % tm == 0
    ch = conv2.shape[1]
    row = lambda i: (i, 0)
    return pl.pallas_call(
        _outproj_kernel,
        grid=(rows // tm,),
        in_specs=[
            pl.BlockSpec((tm, d), row),
            pl.BlockSpec((1, N_MOD, d), _group_map(rows_per_group, tm)),
            pl.BlockSpec((tm, ch), lambda i: (i, 2)),
            pl.BlockSpec((tm, ch), row),
            pl.BlockSpec((tm, ch), row),
            pl.BlockSpec((1, ch), lambda i: (0, 0)),
            pl.BlockSpec((tm, ch), row),
            pl.BlockSpec((tm, ch), row),
            pl.BlockSpec((tm, ch), row),
            pl.BlockSpec((tm, ch), row),
            pl.BlockSpec((tm, ch), row),
            pl.BlockSpec(w_out4.shape, lambda i: (0, 0, 0)),
        ],
        out_specs=pl.BlockSpec((tm, d), row),
        out_shape=jax.ShapeDtypeStruct((rows, d), F32),
        compiler_params=_params(("parallel",), VMEM_LIMIT),
        name="outproj",
    )(x, mod, uc, conv2, z, skip2, gate, hf, hb, yw, yg, w_out4)


def _rmsnorm_kernel(x_ref, g_ref, o_ref):
    x = x_ref[...]
    ms = jnp.mean(x * x, axis=-1, keepdims=True)
    o_ref[...] = x * lax.rsqrt(ms + EPS) * g_ref[...]


def _rmsnorm(x, g):
    rows, d = x.shape
    tm = TOKEN_TILE
    return pl.pallas_call(
        _rmsnorm_kernel,
        grid=(rows // tm,),
        in_specs=[pl.BlockSpec((tm, d), lambda i: (i, 0)), pl.BlockSpec((1, d), lambda i: (0, 0))],
        out_specs=pl.BlockSpec((tm, d), lambda i: (i, 0)),
        out_shape=jax.ShapeDtypeStruct((rows, d), F32),
        compiler_params=_params(("parallel",)),
        name="final_norm",
    )(x, g)


@functools.lru_cache(maxsize=None)
def _rope_angles(n):
    rows = n // GRID_W
    n_freq = HEAD_DIM // 4
    row = np.repeat(np.arange(rows, dtype=np.float32), GRID_W)[:n]
    col = np.tile(np.arange(GRID_W, dtype=np.float32), rows)
    inv = np.power(np.float32(ROPE_BASE), -(np.arange(n_freq, dtype=np.float32) / np.float32(n_freq))).astype(np.float32)
    ang = np.concatenate([row[:, None] * inv, col[:, None] * inv], axis=-1).astype(np.float32).astype(np.float64)
    return np.cos(ang).astype(np.float32), np.sin(ang).astype(np.float32)


def _rope_tables(n):
    cos, sin = (jnp.asarray(a) for a in _rope_angles(n))
    reps = LANES // HEAD_DIM
    cos_t = jnp.tile(jnp.concatenate([cos, cos], axis=-1), (1, reps))
    sin_t = jnp.tile(jnp.concatenate([-sin, sin], axis=-1), (1, reps))
    return cos_t, sin_t


def _lane_rows(vals):
    h = vals.shape[0]
    out = jnp.zeros((SUBLANES, LANES), F32)
    return out.at[:h].set(jnp.broadcast_to(vals.astype(F32)[:, None], (h, LANES)))


def kernel(x, c, ctx, c_ctx, w_mod, b_mod, norm_g, ffn1_w13, ffn1_w2, ffn2_w13, ffn2_w2, w_in, w_out,
           hy_conv_w, hy_conv_b, hy_w1, hy_b1, hy_freq, hy_w2, hy_b2, hy_w3, hy_skip,
           lru_conv_w, lru_conv_b, lru_wa, lru_ba, lru_wx, lru_bx, lru_lambda,
           win_sink, qk_gain, final_g):
    batch, n, d = x.shape
    nc = ctx.shape[1]
    depth = w_mod.shape[0]
    assert batch == 2, "the two batch entries are packed as one complex signal in the long convolution"
    d_hy = hy_skip.shape[-1]
    d_lru = lru_conv_w.shape[-1]
    n_heads_win = win_sink.shape[-1]
    d_q = n_heads_win * HEAD_DIM
    d_kv = d_q // 2
    dims = (3 * d_hy, d_lru, d_q, d_kv)
    assert d_hy == d_lru == d_q and d_hy % LANES == 0

    xl = x.reshape(batch * n, d)
    xc = ctx.reshape(batch * nc, d)

    cond = jnp.zeros((SUBLANES, d), F32).at[:batch].set(c).at[batch].set(c_ctx)
    mods = _modulation(cond, w_mod, b_mod).reshape(depth, SUBLANES, N_MOD, d)

    cos_t, sin_t = _rope_tables(n)
    hsum = jnp.asarray(np.kron(np.eye(LANES // HEAD_DIM), np.full((HEAD_DIM, HEAD_DIM), 1.0 / HEAD_DIM)), dtype=BF16)
    tabs_l = _fft_tables(n)
    tabs_c = _fft_tables(nc)
    no_sink = jnp.full((SUBLANES, LANES), NEG_INF, F32)

    for l in range(depth):
        last = l == depth - 1
        mod_l = mods[l, :batch]
        mod_c = mods[l, batch:batch + 1]
        f1 = _ffn_weights(ffn1_w13[l], ffn1_w2[l])
        f2 = _ffn_weights(ffn2_w13[l], ffn2_w2[l])
        g0, g1, g2 = (norm_g[l, i].reshape(1, d) for i in range(3))

        xl = _ffn(xl, mod_l, g0, *f1, rows_per_group=n, mod_base=0)
        xc = _ffn(xc, mod_c, g0, *f1, rows_per_group=batch * nc, mod_base=0)

        w_in_l = w_in[l].astype(BF16)
        gq = jnp.tile(qk_gain[l, 0], LANES // HEAD_DIM).reshape(1, LANES)
        gk = jnp.tile(qk_gain[l, 1], LANES // HEAD_DIM).reshape(1, LANES)
        pl_ = _mixin(xl, mod_l, g1, w_in_l, gq, gk, hsum, (cos_t, sin_t), n, n, dims)
        pc_ = _mixin(xc, mod_c, g1, w_in_l, gq, gk, hsum, None, batch * nc, nc, dims)
        uin_l, gate_l, qw_l, kw_l, vw_l, qg_l, kg_l, vg_l = pl_
        uin_c, gate_c, qw_c, kw_c, vw_c, qg_c, kg_c, vg_c = pc_

        w4 = jnp.concatenate(
            [jnp.concatenate([jnp.zeros((1, 3 * d_hy), F32), hy_conv_w[l]], axis=0), lru_conv_w[l]], axis=1)
        b4 = jnp.concatenate([hy_conv_b[l], lru_conv_b[l]]).reshape(1, -1)
        uc_l = _shortconv(uin_l, w4, b4, n)
        uc_c = _shortconv(uin_c, w4, b4, nc)

        skip1 = hy_skip[l, 0].reshape(1, d_hy)
        skip2 = hy_skip[l, 1].reshape(1, d_hy)
        spec_l = _filter_spectrum(
            *_hyena_filter(n, hy_w1[l], hy_b1[l], hy_freq[l], hy_w2[l], hy_b2[l], hy_w3[l], d_hy), tabs_l)
        c1 = _long_conv(uc_l[:, :d_hy], spec_l, 0, tabs_l)
        z_l = _hyena_gate(uc_l, 1, c1, uc_l, 0, skip1)
        c2_l = _long_conv(z_l, spec_l, 1, tabs_l)

        wa = _block_diag(lru_wa[l]).astype(BF16)
        wx = _block_diag(lru_wx[l]).astype(BF16)
        lp = jnp.stack([lru_ba[l], lru_bx[l], lru_lambda[l]], axis=1)
        lru_col = 3 * d_hy // d_lru
        hf_c, hb_c, h_last = _lru(uc_c, lru_col, wa, wx, lp, jnp.zeros((batch, 2, d_lru), F32), batch, nc)
        hf_l, hb_l, _ = _lru(uc_l, lru_col, wa, wx, lp, h_last, batch, n)

        sink_rows = _lane_rows(win_sink[l] * LOG2E)
        yw_l = _window_attention(qw_l, kw_l, vw_l, kw_c, vw_c, sink_rows, batch, n, nc)

        yg_l = _global_attention(qg_l, kg_l, vg_l, kg_c, vg_c, batch, n, nc)

        w_out4 = w_out[l].astype(BF16).reshape(4, d_hy, d)
        xl = _outproj(xl, mod_l, uc_l, c2_l, z_l, skip2, gate_l, hf_l, hb_l, yw_l, yg_l, w_out4, n)

        if not last:
            spec_c = _filter_spectrum(
                *_hyena_filter(nc, hy_w1[l], hy_b1[l], hy_freq[l], hy_w2[l], hy_b2[l], hy_w3[l], d_hy), tabs_c)
            cc1 = _long_conv(uc_c[:, :d_hy], spec_c, 0, tabs_c)
            z_c = _hyena_gate(uc_c, 1, cc1, uc_c, 0, skip1)
            c2_c = _long_conv(z_c, spec_c, 1, tabs_c)
            yw_c = _dense_attention(qw_c, kw_c, vw_c, sink_rows, batch, nc)
            yg_c = _dense_attention(qg_c, kg_c, vg_c, no_sink, batch, nc)
            xc = _outproj(xc, mod_c, uc_c, c2_c, z_c, skip2, gate_c, hf_c, hb_c, yw_c, yg_c, w_out4, batch * nc)

        xl = _ffn(xl, mod_l, g2, *f2, rows_per_group=n, mod_base=6)
        if not last:
            xc = _ffn(xc, mod_c, g2, *f2, rows_per_group=batch * nc, mod_base=6)

    return _rmsnorm(xl, final_g.reshape(1, d)).reshape(batch, n, d)
```

```python
import functools
import math

import numpy as np
import jax
import jax.numpy as jnp
from jax import lax
from jax.experimental import pallas as pl
from jax.experimental.pallas import tpu as pltpu

F32 = jnp.float32
BF16 = jnp.bfloat16

HEAD_DIM = 64
GRID_W = 64
N_MOD = 9
WINDOW = 128
Q_BLOCK = 128
HY_ORDER = 2
HY_BANDS = 16
HY_MIN_DECAY = math.log(1e-2) / 1.5
HY_MAX_DECAY = math.log(1e-2) / 0.3
LRU_BLOCKS = 4
LRU_C = 8.0
ROPE_BASE = 10000.0
EPS = 1e-6
NEG_INF = -1e30
LOG2E = math.log2(math.e)
Q_SCALE = HEAD_DIM ** -0.5 * LOG2E

LANES = 128
SUBLANES = 8
VMEM_LIMIT = 56 * 1024 * 1024

TOKEN_TILE = 512
FF_CHUNK = 256
CONV_TILE = 256
SCAN_TILE = 256
FLASH_TQ = 512
FLASH_TK = 1024
WIN_TQ = 512
FFT_COL_TILE = 1024


def _params(sem, vmem=None):
    return pltpu.CompilerParams(dimension_semantics=sem, vmem_limit_bytes=vmem)


def _split_bf16(x):
    hi = x.astype(BF16)
    lo = (x - hi.astype(F32)).astype(BF16)
    return hi, lo


def _dot(a, b):
    return jnp.dot(a, b, preferred_element_type=F32)


def _dot_nt(a, b):
    return lax.dot_general(a, b, (((1,), (1,)), ((), ())), preferred_element_type=F32)


def _dot3_const(ch, cl, x):
    xh, xl = _split_bf16(x)
    return _dot(ch, xh) + (_dot(cl, xh) + _dot(ch, xl))


def _mod_kernel(c_ref, w_ref, b_ref, o_ref):
    c = c_ref[...]
    a = c * jax.nn.sigmoid(c)
    ah, al = _split_bf16(a)
    wh, wl = _split_bf16(w_ref[0])
    o_ref[0] = _dot(ah, wh) + (_dot(al, wh) + _dot(ah, wl)) + b_ref[0]


def _modulation(cond, w_mod, b_mod):
    depth, d, nd = w_mod.shape
    tn = 1152
    assert nd % tn == 0
    return pl.pallas_call(
        _mod_kernel,
        grid=(depth, nd // tn),
        in_specs=[
            pl.BlockSpec((SUBLANES, d), lambda l, j: (0, 0)),
            pl.BlockSpec((1, d, tn), lambda l, j: (l, 0, j)),
            pl.BlockSpec((1, 1, tn), lambda l, j: (l, 0, j)),
        ],
        out_specs=pl.BlockSpec((1, SUBLANES, tn), lambda l, j: (l, 0, j)),
        out_shape=jax.ShapeDtypeStruct((depth, SUBLANES, nd), F32),
        compiler_params=_params(("parallel", "parallel")),
        name="modulation",
    )(cond, w_mod, b_mod.reshape(depth, 1, nd))


def _adaln(x, g, shift, scale):
    ms = jnp.mean(x * x, axis=-1, keepdims=True)
    y = x * lax.rsqrt(ms + EPS) * g
    return y * (1.0 + scale) + shift


def _group_map(rows_per_group, tile):
    tiles = rows_per_group // tile
    return lambda i: (i // tiles, 0, 0)


def _ffn_kernel(x_ref, mod_ref, g_ref, w1_ref, w3_ref, w2_ref, o_ref, xn_ref, acc_ref, *, mod_base, n_chunks):
    x = x_ref[...]
    shift = mod_ref[0, mod_base:mod_base + 1, :]
    scale = mod_ref[0, mod_base + 1:mod_base + 2, :]
    gate = mod_ref[0, mod_base + 2:mod_base + 3, :]
    xn_ref[...] = _adaln(x, g_ref[...], shift, scale).astype(BF16)
    acc_ref[...] = jnp.zeros_like(acc_ref)

    def body(c, carry):
        xn = xn_ref[...]
        h = _dot(xn, w1_ref[c])
        u = _dot(xn, w3_ref[c])
        a = (h * jax.nn.sigmoid(h) * u).astype(BF16)
        acc_ref[...] += _dot(a, w2_ref[c])
        return carry

    lax.fori_loop(0, n_chunks, body, 0, unroll=True)
    o_ref[...] = x + 0.5 * gate * acc_ref[...]


def _ffn(x, mod, g, w1c, w3c, w2c, rows_per_group, mod_base):
    rows, d = x.shape
    tm = TOKEN_TILE
    assert rows % tm == 0 and rows_per_group % tm == 0
    n_chunks, _, ch = w1c.shape
    const3 = lambda i: (0, 0, 0)
    return pl.pallas_call(
        functools.partial(_ffn_kernel, mod_base=mod_base, n_chunks=n_chunks),
        grid=(rows // tm,),
        in_specs=[
            pl.BlockSpec((tm, d), lambda i: (i, 0)),
            pl.BlockSpec((1, N_MOD, d), _group_map(rows_per_group, tm)),
            pl.BlockSpec((1, d), lambda i: (0, 0)),
            pl.BlockSpec((n_chunks, d, ch), const3, pipeline_mode=pl.Buffered(1)),
            pl.BlockSpec((n_chunks, d, ch), const3, pipeline_mode=pl.Buffered(1)),
            pl.BlockSpec((n_chunks, ch, d), const3, pipeline_mode=pl.Buffered(1)),
        ],
        out_specs=pl.BlockSpec((tm, d), lambda i: (i, 0)),
        out_shape=jax.ShapeDtypeStruct((rows, d), F32),
        scratch_shapes=[pltpu.VMEM((tm, d), BF16), pltpu.VMEM((tm, d), F32)],
        compiler_params=_params(("parallel",), VMEM_LIMIT),
        name="ffn",
    )(x, mod, g, w1c, w3c, w2c)


def _ffn_weights(w13, w2):
    d, two_ff = w13.shape
    ff = two_ff // 2
    assert ff % FF_CHUNK == 0
    nc = ff // FF_CHUNK
    w1c = w13[:, :ff].astype(BF16).reshape(d, nc, FF_CHUNK).transpose(1, 0, 2)
    w3c = w13[:, ff:].astype(BF16).reshape(d, nc, FF_CHUNK).transpose(1, 0, 2)
    w2c = w2.astype(BF16).reshape(nc, FF_CHUNK, d)
    return w1c, w3c, w2c


def _swap_halves(z):
    lane = lax.broadcasted_iota(jnp.int32, z.shape, 1)
    first = (lane % HEAD_DIM) < (HEAD_DIM // 2)
    return jnp.where(first, pltpu.roll(z, LANES - HEAD_DIM // 2, 1), pltpu.roll(z, HEAD_DIM // 2, 1))


def _mixin_kernel(*refs, use_rope, tiles_per_seq, d_hy3, d_lru, d_q, d_kv):
    if use_rope:
        (x_ref, mod_ref, g_ref, w_ref, gq_ref, gk_ref, hs_ref, cos_ref, sin_ref, xp_ref, xq_ref, cw_ref, cb_ref,
         uin_ref, gate_ref, qw_ref, kw_ref, vw_ref, qg_ref, kg_ref, vg_ref, ext_ref) = refs
    else:
        (x_ref, mod_ref, g_ref, w_ref, gq_ref, gk_ref, hs_ref,
         uin_ref, gate_ref, qw_ref, kw_ref, vw_ref, qg_ref, kg_ref, vg_ref) = refs
        cos_ref = sin_ref = None
    x = x_ref[...]
    shift, scale_mod = mod_ref[0, 3:4, :], mod_ref[0, 4:5, :]
    xn = _adaln(x, g_ref[...], shift, scale_mod).astype(BF16)

    def proj(lo, width):
        return _dot(xn, w_ref[:, lo:lo + width])

    def rope(z):
        if not use_rope:
            return z
        return z * cos_ref[...] + _swap_halves(z) * sin_ref[...]

    def head_norm(z, gain):
        z2 = z * z
        zh, zl = _split_bf16(z2)
        ms = _dot(zh, hs_ref[...]) + _dot(zl, hs_ref[...])
        return z * lax.rsqrt(ms + EPS) * gain

    def store_values(v_ref, v):
        assert d_kv == LANES
        low = lax.broadcasted_iota(jnp.int32, v.shape, 1) < HEAD_DIM
        v_ref[:, 0:LANES] = jnp.where(low, v, 1.0).astype(BF16)
        v_ref[:, LANES:2 * LANES] = jnp.where(low, pltpu.roll(v, HEAD_DIM, 1), 1.0).astype(BF16)

    off = 0
    d_conv = d_hy3 + d_lru
    if use_rope:
        tm = x.shape[0]
        j = pl.program_id(0) % tiles_per_seq
        keep_prev = (j > 0).astype(F32)
        keep_next = (j < tiles_per_seq - 1).astype(F32)
        halo = jnp.concatenate([xp_ref[...], xq_ref[...]], axis=0)
        hn = _adaln(halo, g_ref[...], shift, scale_mod).astype(BF16)
        hp = _dot(hn, w_ref[:, 0:d_conv])
        u = proj(off, d_conv)
        ext_ref[0:SUBLANES, :] = hp[0:SUBLANES] * keep_prev
        ext_ref[SUBLANES:SUBLANES + tm, :] = u
        ext_ref[SUBLANES + tm:, :] = hp[SUBLANES:] * keep_next
        acc = cb_ref[...] + cw_ref[2:3, :] * u
        for k in (0, 1, 3):
            acc = acc + cw_ref[k:k + 1, :] * ext_ref[SUBLANES - 2 + k:SUBLANES - 2 + k + tm, :]
        uin_ref[...] = acc
    else:
        uin_ref[...] = proj(off, d_conv)
    off += d_conv
    gate_ref[...] = proj(off, d_lru)
    off += d_lru
    scale = Q_SCALE
    z = proj(off, d_q)
    for c in range(d_q // LANES):
        qw_ref[:, c * LANES:(c + 1) * LANES] = (rope(z[:, c * LANES:(c + 1) * LANES]) * scale).astype(BF16)
    off += d_q
    z = proj(off, 2 * d_kv)
    kw_ref[...] = rope(z[:, :d_kv]).astype(BF16)
    store_values(vw_ref, z[:, d_kv:])
    off += 2 * d_kv
    z = proj(off, d_q)
    for c in range(d_q // LANES):
        zn = head_norm(z[:, c * LANES:(c + 1) * LANES], gq_ref[...])
        qg_ref[:, c * LANES:(c + 1) * LANES] = (rope(zn) * scale).astype(BF16)
    off += d_q
    z = proj(off, 2 * d_kv)
    kg_ref[...] = rope(head_norm(z[:, :d_kv], gk_ref[...])).astype(BF16)
    store_values(vg_ref, z[:, d_kv:])


def _mixin(x, mod, g, w_in, gq, gk, hsum, rope_tabs, conv, rows_per_group, seq_len, dims):
    rows, d = x.shape
    d_hy3, d_lru, d_q, d_kv = dims
    tm = TOKEN_TILE
    assert rows % tm == 0 and rows_per_group % tm == 0
    use_rope = rope_tabs is not None
    tiles_per_seq = seq_len // tm
    d_in = w_in.shape[1]
    row = lambda i: (i, 0)
    const2 = lambda i: (0, 0)
    in_specs = [
        pl.BlockSpec((tm, d), row),
        pl.BlockSpec((1, N_MOD, d), _group_map(rows_per_group, tm)),
        pl.BlockSpec((1, d), const2),
        pl.BlockSpec((d, d_in), const2, pipeline_mode=pl.Buffered(1)),
        pl.BlockSpec((1, LANES), const2),
        pl.BlockSpec((1, LANES), const2),
        pl.BlockSpec((LANES, LANES), const2),
    ]
    args = [x, mod, g, w_in, gq, gk, hsum]
    scratch = []
    if use_rope:
        assert seq_len % tm == 0
        tab = lambda i: (i % tiles_per_seq, 0)
        sub = tm // SUBLANES
        last = rows // SUBLANES - 1
        w4, b4 = conv
        in_specs += [pl.BlockSpec((tm, LANES), tab), pl.BlockSpec((tm, LANES), tab),
                     pl.BlockSpec((SUBLANES, d), lambda i: (jnp.maximum(i * sub - 1, 0), 0)),
                     pl.BlockSpec((SUBLANES, d), lambda i: (jnp.minimum((i + 1) * sub, last), 0)),
                     pl.BlockSpec(w4.shape, const2), pl.BlockSpec(b4.shape, const2)]
        args += list(rope_tabs) + [x, x, w4, b4]
        scratch = [pltpu.VMEM((tm + 2 * SUBLANES, d_hy3 + d_lru), F32)]
    widths = (d_hy3 + d_lru, d_lru, d_q, d_kv, 2 * d_kv, d_q, d_kv, 2 * d_kv)
    dtypes = (F32, F32, BF16, BF16, BF16, BF16, BF16, BF16)
    return pl.pallas_call(
        functools.partial(_mixin_kernel, use_rope=use_rope, tiles_per_seq=tiles_per_seq,
                          d_hy3=d_hy3, d_lru=d_lru, d_q=d_q, d_kv=d_kv),
        grid=(rows // tm,),
        in_specs=in_specs,
        out_specs=[pl.BlockSpec((tm, w), row) for w in widths],
        out_shape=[jax.ShapeDtypeStruct((rows, w), dt) for w, dt in zip(widths, dtypes)],
        scratch_shapes=scratch,
        compiler_params=_params(("parallel",), VMEM_LIMIT),
        name="mixin_rope" if use_rope else "mixin",
    )(*args)


def _shortconv_kernel(cur_ref, prev_ref, next_ref, w_ref, b_ref, o_ref, ext_ref, *, tiles_per_seq):
    ts = cur_ref.shape[0]
    j = pl.program_id(0) % tiles_per_seq
    keep_prev = (j > 0).astype(F32)
    keep_next = (j < tiles_per_seq - 1).astype(F32)
    ext_ref[0:SUBLANES, :] = prev_ref[...] * keep_prev
    ext_ref[SUBLANES:SUBLANES + ts, :] = cur_ref[...]
    ext_ref[SUBLANES + ts:, :] = next_ref[...] * keep_next
    acc = b_ref[...] + w_ref[2:3, :] * cur_ref[...]
    for k in (0, 1, 3):
        acc = acc + w_ref[k:k + 1, :] * ext_ref[SUBLANES - 2 + k:SUBLANES - 2 + k + ts, :]
    o_ref[...] = acc


def _shortconv(u, w4, bias, seq_len):
    rows, ch = u.shape
    ts = min(CONV_TILE, seq_len)
    assert seq_len % ts == 0 and rows % ts == 0 and ts % SUBLANES == 0
    tiles_per_seq = seq_len // ts
    sub = ts // SUBLANES
    last = rows // SUBLANES - 1
    return pl.pallas_call(
        functools.partial(_shortconv_kernel, tiles_per_seq=tiles_per_seq),
        grid=(rows // ts,),
        in_specs=[
            pl.BlockSpec((ts, ch), lambda i: (i, 0)),
            pl.BlockSpec((SUBLANES, ch), lambda i: (jnp.maximum(i * sub - 1, 0), 0)),
            pl.BlockSpec((SUBLANES, ch), lambda i: (jnp.minimum((i + 1) * sub, last), 0)),
            pl.BlockSpec((4, ch), lambda i: (0, 0)),
            pl.BlockSpec((1, ch), lambda i: (0, 0)),
        ],
        out_specs=pl.BlockSpec((ts, ch), lambda i: (i, 0)),
        out_shape=jax.ShapeDtypeStruct((rows, ch), F32),
        scratch_shapes=[pltpu.VMEM((ts + 2 * SUBLANES, ch), F32)],
        compiler_params=_params(("parallel",)),
        name="shortconv",
    )(u, u, u, w4, bias)


def _fft_split(n):
    big = 2 * n
    n2 = LANES if big >= 16 * LANES else LANES // 4
    assert big % n2 == 0 and (big // n2) % 16 == 0
    return big // n2, n2


def _np_hi_lo(m):
    m32 = jnp.asarray(np.asarray(m, dtype=np.float32))
    hi = m32.astype(BF16)
    lo = (m32 - hi.astype(F32)).astype(BF16)
    return hi, lo


@functools.lru_cache(maxsize=None)
def _fft_tables(n):
    n1, n2 = _fft_split(n)
    big = n1 * n2
    h1 = n1 // 2
    f1 = np.arange(n1)[:, None].astype(np.float64)
    s1 = np.arange(h1)[None, :].astype(np.float64)
    th = 2.0 * np.pi * f1 * s1 / n1
    c, s = np.cos(th), np.sin(th)
    w1 = np.block([[c, s], [-s, c]])
    w3 = np.block([[c.T, -s.T], [s.T, c.T]]) / big
    s1f = np.arange(n1)[None, :].astype(np.float64)
    thf = 2.0 * np.pi * f1 * s1f / n1
    w1f = np.concatenate([np.cos(thf), -np.sin(thf)], axis=0)
    f2 = np.arange(n2)[:, None].astype(np.float64)
    s2 = np.arange(n2)[None, :].astype(np.float64)
    th2 = 2.0 * np.pi * f2 * s2 / n2
    eye = np.eye(LANES // n2)
    c2, sn2 = np.kron(eye, np.cos(th2)), np.kron(eye, np.sin(th2))
    m2 = np.block([[c2, -sn2], [sn2, c2]])
    ph = 2.0 * np.pi * np.arange(n1)[:, None] * np.arange(n2)[None, :] / big
    tw = np.stack([np.cos(ph), -np.sin(ph)], axis=0)
    tw = np.tile(tw, (1, 1, LANES // n2)).astype(np.float32)
    return dict(n1=n1, n2=n2, w1=w1, w3=w3, w1f=w1f, m2=m2, m2t=m2.T, tw=tw)


def _dot3_rconst(x, ch, cl):
    xh, xl = _split_bf16(x)
    return _dot(xh, ch) + (_dot(xl, ch) + _dot(xh, cl))


def _fftconv_kernel(*refs, fwd_only, n1, groups):
    if fwd_only:
        v_ref, inv_ref, tw_ref, w1h_ref, w1l_ref, mh_ref, ml_ref, o_ref = refs
    else:
        (v_ref, sr_ref, si_ref, tw_ref, w1h_ref, w1l_ref, mh_ref, ml_ref, mth_ref, mtl_ref,
         w3h_ref, w3l_ref, o_ref) = refs

    def stack(x):
        return jnp.concatenate([x[:, g * LANES:(g + 1) * LANES] for g in range(groups)], axis=0)

    def unstack(x):
        return jnp.concatenate([x[g * n1:(g + 1) * n1] for g in range(groups)], axis=1)

    a = _dot3_const(w1h_ref[...], w1l_ref[...], v_ref[...])
    twr = jnp.concatenate([tw_ref[0]] * groups, axis=0)
    twi = jnp.concatenate([tw_ref[1]] * groups, axis=0)
    ar, ai = stack(a[:n1]), stack(a[n1:])
    b = jnp.concatenate([ar * twr - ai * twi, ar * twi + ai * twr], axis=1)
    x = _dot3_rconst(b, mh_ref[...], ml_ref[...])
    xr, xi = x[:, :LANES], x[:, LANES:]
    if fwd_only:
        o_ref[0:n1, :] = unstack(xr) * inv_ref[...]
        o_ref[n1:2 * n1, :] = unstack(xi) * inv_ref[...]
        return
    kr, ki = stack(sr_ref[...]), stack(si_ref[...])
    y = jnp.concatenate([xr * kr - xi * ki, xr * ki + xi * kr], axis=1)
    c = _dot3_rconst(y, mth_ref[...], mtl_ref[...])
    cr, ci = c[:, :LANES], c[:, LANES:]
    d = jnp.concatenate([unstack(cr * twr + ci * twi), unstack(ci * twr - cr * twi)], axis=0)
    o_ref[...] = _dot3_const(w3h_ref[...], w3l_ref[...], d)


def _fftconv(v2, spec, order, tabs, fwd_only):
    n1 = tabs["n1"]
    cols = v2.shape[1]
    tc = min(FFT_COL_TILE, cols)
    assert cols % tc == 0 and v2.shape[0] == n1
    nblk = cols // tc
    const2 = lambda j: (0, 0)
    const3 = lambda j: (0, 0, 0)
    w1h, w1l = _np_hi_lo(tabs["w1f"] if fwd_only else tabs["w1"])
    mh, ml = _np_hi_lo(tabs["m2"])
    in_specs = [pl.BlockSpec((n1, tc), lambda j: (0, j))]
    args = [v2]
    if fwd_only:
        in_specs.append(pl.BlockSpec((1, tc), lambda j: (0, j)))
        args.append(spec)
    else:
        in_specs += [pl.BlockSpec((n1, tc), lambda j: (0, order * nblk + j)),
                     pl.BlockSpec((n1, tc), lambda j: (1, order * nblk + j))]
        args += [spec, spec]
    in_specs += [pl.BlockSpec((2, n1, LANES), const3), pl.BlockSpec(w1h.shape, const2), pl.BlockSpec(w1h.shape, const2),
                 pl.BlockSpec(mh.shape, const2), pl.BlockSpec(mh.shape, const2)]
    args += [jnp.asarray(tabs["tw"]), w1h, w1l, mh, ml]
    if not fwd_only:
        mth, mtl = _np_hi_lo(tabs["m2t"])
        w3h, w3l = _np_hi_lo(tabs["w3"])
        in_specs += [pl.BlockSpec(mh.shape, const2), pl.BlockSpec(mh.shape, const2),
                     pl.BlockSpec(w3h.shape, const2), pl.BlockSpec(w3h.shape, const2)]
        args += [mth, mtl, w3h, w3l]
    out_rows = 2 * n1 if fwd_only else n1
    return pl.pallas_call(
        functools.partial(_fftconv_kernel, fwd_only=fwd_only, n1=n1, groups=tc // LANES),
        grid=(nblk,),
        in_specs=in_specs,
        out_specs=pl.BlockSpec((out_rows, tc), lambda j: (0, j)),
        out_shape=jax.ShapeDtypeStruct((out_rows, cols), F32),
        compiler_params=_params(("parallel",), VMEM_LIMIT),
        name="fft_filter" if fwd_only else "fft_conv",
    )(*args)


def _to_fft_layout(v, n1, n2):
    ch = v.shape[1]
    return v.reshape(n1, n2, ch).transpose(0, 2, 1).reshape(n1, ch * n2)


def _from_fft_layout(y, n2, ch):
    rows = y.shape[0]
    return y.reshape(rows, ch, n2).transpose(0, 2, 1).reshape(rows * n2, ch)


def _filter_spectrum(k, norm, tabs):
    inv = jnp.repeat(1.0 / norm, tabs["n2"], axis=1)
    return _fftconv(_to_fft_layout(k, tabs["n1"], tabs["n2"]), inv, 0, tabs, True)


def _long_conv(v, spec, order, tabs):
    n1, n2 = tabs["n1"], tabs["n2"]
    y = _fftconv(_to_fft_layout(v, n1, n2), spec, order, tabs, False)
    return _from_fft_layout(y, n2, v.shape[1])


FILTER_FEATS = 64


@functools.lru_cache(maxsize=None)
def _filter_features(n):
    r = np.arange(2 * n)
    pos = np.where(r < n, r, np.where(r == n, 0, 2 * n - r))
    t = np.linspace(0.0, 1.0, n, dtype=np.float32)[pos]
    w = (np.float32(2.0 * math.pi) * pos.astype(np.float32)) / np.float32(n)
    f = np.linspace(1e-4, HY_BANDS - 1, HY_BANDS, dtype=np.float32)
    fw = (f[None, :] * w[:, None]).astype(np.float32).astype(np.float64)
    z = np.zeros((2 * n, FILTER_FEATS), np.float32)
    z[:, 0] = t
    z[:, 1:1 + HY_BANDS] = np.cos(fw)
    z[:, 1 + HY_BANDS:1 + 2 * HY_BANDS] = -np.sin(fw)
    return z


def _dot3(a, b):
    ah, al = _split_bf16(a)
    bh, bl = _split_bf16(b)
    return _dot(ah, bh) + (_dot(al, bh) + _dot(ah, bl))


def _filter_kernel(z_ref, w1_ref, b1_ref, fr_ref, w2_ref, b2_ref, w3_ref, dl_ref, k_ref, sum_ref, *, n):
    i = pl.program_id(0)
    half = z_ref.shape[0]
    cols = dl_ref.shape[1]
    z = z_ref[...]
    h = jnp.sin(fr_ref[0:1, :] * (_dot3(z, w1_ref[...]) + b1_ref[...]))
    h = jnp.sin(fr_ref[1:2, :] * (_dot3(h, w2_ref[...]) + b2_ref[...]))
    k2 = _dot3(h, w3_ref[0])

    @pl.when(i == 0)
    def _():
        sum_ref[...] = jnp.zeros_like(sum_ref)

    row = 2 * i * half + lax.broadcasted_iota(jnp.int32, (half, 1), 0)
    for p in range(2):
        t = z[:, p * FILTER_FEATS:p * FILTER_FEATS + 1]
        k = k2[:, p * cols:(p + 1) * cols] * jnp.exp(-t * dl_ref[...])
        sum_ref[...] += jnp.sum(jnp.abs(k), axis=0, keepdims=True)
        k_ref[p * half:(p + 1) * half, :] = jnp.where(row + p * half == n, 0.0, k)


def _pair_diag(w):
    return jnp.kron(jnp.eye(2, dtype=w.dtype), w)


def _hyena_filter(n, w1, b1, freq, w2, b2, w3, d_hy):
    hid = w1.shape[1]
    cols = HY_ORDER * d_hy
    tile = min(TOKEN_TILE, n)
    assert n % tile == 0 and 2 * FILTER_FEATS == LANES
    tiles_per_dir = n // tile
    half = tile // 2
    z = _filter_features(n).reshape(2 * tiles_per_dir, 2, half, FILTER_FEATS)
    z = jnp.asarray(np.ascontiguousarray(z.transpose(0, 2, 1, 3)).reshape(2 * tiles_per_dir * half, LANES))
    w1p = _pair_diag(jnp.zeros((FILTER_FEATS, hid), F32).at[:w1.shape[0]].set(w1))
    w3d = w3.reshape(hid, HY_ORDER, 2, d_hy).transpose(2, 0, 1, 3).reshape(2, hid, cols)
    w3p = jnp.stack([_pair_diag(w3d[0]), _pair_diag(w3d[1])])
    deltas = jnp.abs(jnp.linspace(HY_MIN_DECAY, HY_MAX_DECAY, d_hy, dtype=F32))
    dl = jnp.tile(deltas, HY_ORDER).reshape(1, cols)
    pair = lambda v: jnp.tile(v.reshape(-1, hid), (1, 2))
    const2 = lambda i: (0, 0)
    k, sums = pl.pallas_call(
        functools.partial(_filter_kernel, n=n),
        grid=(2 * tiles_per_dir,),
        in_specs=[
            pl.BlockSpec((half, LANES), lambda i: (i, 0)),
            pl.BlockSpec((LANES, 2 * hid), const2),
            pl.BlockSpec((1, 2 * hid), const2),
            pl.BlockSpec((2, 2 * hid), const2),
            pl.BlockSpec((2 * hid, 2 * hid), const2),
            pl.BlockSpec((1, 2 * hid), const2),
            pl.BlockSpec((1, 2 * hid, 2 * cols), lambda i: (i // tiles_per_dir, 0, 0)),
            pl.BlockSpec((1, cols), const2),
        ],
        out_specs=[pl.BlockSpec((tile, cols), lambda i: (i, 0)), pl.BlockSpec((1, cols), const2)],
        out_shape=[jax.ShapeDtypeStruct((2 * n, cols), F32), jax.ShapeDtypeStruct((1, cols), F32)],
        compiler_params=_params(("arbitrary",)),
        name="hyena_filter",
    )(z, w1p, pair(b1), pair(freq), _pair_diag(w2), pair(b2), w3p, dl)
    return k, sums + EPS


def _gate_kernel(a_ref, b_ref, c_ref, s_ref, o_ref):
    o_ref[...] = a_ref[...] * (b_ref[...] + c_ref[...] * s_ref[...])


def _hyena_gate(uc, a_col, conv, c_src, c_col, skip):
    rows, ch = conv.shape
    tm = min(TOKEN_TILE, rows)
    assert rows % tm == 0
    return pl.pallas_call(
        _gate_kernel,
        grid=(rows // tm,),
        in_specs=[
            pl.BlockSpec((tm, ch), lambda i: (i, a_col)),
            pl.BlockSpec((tm, ch), lambda i: (i, 0)),
            pl.BlockSpec((tm, ch), lambda i: (i, c_col)),
            pl.BlockSpec((1, ch), lambda i: (0, 0)),
        ],
        out_specs=pl.BlockSpec((tm, ch), lambda i: (i, 0)),
        out_shape=jax.ShapeDtypeStruct((rows, ch), F32),
        compiler_params=_params(("parallel",)),
        name="hyena_gate",
    )(uc, conv, c_src, skip)


def _sigmoid(x):
    return 0.5 * jnp.tanh(0.5 * x) + 0.5


def _shift_rows(x, s, fill, reverse):
    row = lax.broadcasted_iota(jnp.int32, x.shape, 1)
    if reverse:
        return jnp.where(row >= SUBLANES - s, fill, pltpu.roll(x, SUBLANES - s, 1))
    return jnp.where(row < s, fill, pltpu.roll(x, s, 1))


def _lru_kernel(uf_ref, ub_ref, wa_ref, wx_ref, p_ref, h0_ref, hf_ref, hb_ref, hl_ref, carry_ref):
    j = pl.program_id(1)
    nt = pl.num_programs(1)
    tl = uf_ref.shape[0]

    @pl.when(j == 0)
    def _():
        carry_ref[...] = h0_ref[0]

    for d, (u_ref, o_ref) in enumerate(((uf_ref, hf_ref), (ub_ref, hb_ref))):
        reverse = d == 1
        u = u_ref[...]
        ub16 = u.astype(BF16)
        r = _sigmoid(_dot(ub16, wa_ref[d]) + p_ref[d, 0:1, :])
        gi = _sigmoid(_dot(ub16, wx_ref[d]) + p_ref[d, 1:2, :])
        nlam = -p_ref[d, 2:3, :]
        softplus = jnp.maximum(nlam, 0.0) + jnp.log1p(jnp.exp(-jnp.abs(nlam)))
        log_a = -LRU_C * r * softplus
        a = jnp.exp(log_a)
        b = jnp.sqrt(-jnp.tanh(log_a) * (a * a + 1.0)) * gi * u
        a = a.reshape(tl // SUBLANES, SUBLANES, a.shape[1])
        b = b.reshape(a.shape)
        s = 1
        while s < SUBLANES:
            a_s = _shift_rows(a, s, 1.0, reverse)
            b_s = _shift_rows(b, s, 0.0, reverse)
            b = a * b_s + b
            a = a * a_s
            s *= 2
        carry = carry_ref[d:d + 1, :]
        groups = range(tl // SUBLANES)
        for g in (reversed(groups) if reverse else groups):
            hg = b[g] + a[g] * carry
            o_ref[g * SUBLANES:(g + 1) * SUBLANES, :] = hg
            carry = hg[0:1, :] if reverse else hg[SUBLANES - 1:SUBLANES, :]
        carry_ref[d:d + 1, :] = carry

    @pl.when(j == nt - 1)
    def _():
        hl_ref[0] = carry_ref[...]


def _lru(uc, col, wa, wx, p, h0, batch, seq_len):
    ch = wa.shape[-1]
    tl = min(SCAN_TILE, seq_len)
    assert seq_len % tl == 0
    nt = seq_len // tl
    rows = batch * seq_len
    fwd = lambda b, j: (b * nt + j, col)
    bwd = lambda b, j: (b * nt + nt - 1 - j, col)
    fwd_o = lambda b, j: (b * nt + j, 0)
    bwd_o = lambda b, j: (b * nt + nt - 1 - j, 0)
    const3 = lambda b, j: (0, 0, 0)
    return pl.pallas_call(
        _lru_kernel,
        grid=(batch, nt),
        in_specs=[
            pl.BlockSpec((tl, ch), fwd),
            pl.BlockSpec((tl, ch), bwd),
            pl.BlockSpec((2, ch, ch), const3),
            pl.BlockSpec((2, ch, ch), const3),
            pl.BlockSpec((2, 3, ch), const3),
            pl.BlockSpec((1, 2, ch), lambda b, j: (b, 0, 0)),
        ],
        out_specs=[
            pl.BlockSpec((tl, ch), fwd_o),
            pl.BlockSpec((tl, ch), bwd_o),
            pl.BlockSpec((1, 2, ch), lambda b, j: (b, 0, 0)),
        ],
        out_shape=[
            jax.ShapeDtypeStruct((rows, ch), F32),
            jax.ShapeDtypeStruct((rows, ch), F32),
            jax.ShapeDtypeStruct((batch, 2, ch), F32),
        ],
        scratch_shapes=[pltpu.VMEM((2, ch), F32)],
        compiler_params=_params(("parallel", "arbitrary")),
        name="rglru",
    )(uc, uc, wa, wx, p, h0)


def _block_diag(w):
    two, nb, d, _ = w.shape
    eye = jnp.eye(nb, dtype=w.dtype)
    return jnp.einsum('xnde,nm->xndme', w, eye).reshape(two, nb * d, nb * d)


def _expand_q(q_ref, qe_ref):
    tq = q_ref.shape[0]
    half = LANES // 2
    low = lax.broadcasted_iota(jnp.int32, (tq, LANES), 1) < half
    for kv in range(2):
        qf = q_ref[:, kv * LANES:(kv + 1) * LANES].astype(F32)
        qr = pltpu.roll(qf, half, 1)
        if kv == 0:
            g0, g1 = jnp.where(low, qf, 0.0), jnp.where(low, qr, 0.0)
        else:
            g0, g1 = jnp.where(low, 0.0, qr), jnp.where(low, 0.0, qf)
        qe_ref[kv, 0:tq, :] = g0.astype(BF16)
        qe_ref[kv, tq:2 * tq, :] = g1.astype(BF16)


def _row_max(s):
    m = s[:, 0:LANES]
    for c in range(1, s.shape[1] // LANES):
        m = jnp.maximum(m, s[:, c * LANES:(c + 1) * LANES])
    return jnp.max(m, axis=-1, keepdims=True)


def _exp2_bf16(s, m):
    return jnp.concatenate(
        [jnp.exp2(s[:, c * LANES:(c + 1) * LANES] - m).astype(BF16) for c in range(s.shape[1] // LANES)], axis=1)


def _finish_heads(acc, tq):
    half = LANES // 2
    o = acc / pltpu.roll(acc, half, 1)
    low = lax.broadcasted_iota(jnp.int32, (tq, LANES), 1) < half
    return jnp.where(low, o[0:tq], pltpu.roll(o[tq:2 * tq], half, 1))


def _win_kernel(q_ref, kp_ref, kc_ref, kn_ref, vp_ref, vc_ref, vn_ref, kx_ref, vx_ref, sink_ref, o_ref, qe_ref,
                *, seq_len):
    i = pl.program_id(1)
    tq = q_ref.shape[0]
    halo = kp_ref.shape[0]
    _expand_q(q_ref, qe_ref)
    kk = jnp.concatenate([kp_ref[...], kc_ref[...], kn_ref[...]], axis=0)
    vv = jnp.concatenate([vp_ref[...], vc_ref[...], vn_ref[...]], axis=0)
    n_loc = tq + 2 * halo
    row = lax.broadcasted_iota(jnp.int32, (tq, n_loc), 0)
    col = lax.broadcasted_iota(jnp.int32, (tq, n_loc), 1)
    kpos = i * tq - halo + col
    valid = (jnp.abs(col - halo - row) <= WINDOW) & (kpos >= 0) & (kpos < seq_len)
    valid = jnp.concatenate([valid, valid], axis=0)
    high = lax.broadcasted_iota(jnp.int32, (2 * tq, LANES), 1) >= LANES // 2
    for kv in range(2):
        vsl = slice(kv * LANES, (kv + 1) * LANES)
        s_loc = jnp.where(valid, _dot_nt(qe_ref[kv], kk), NEG_INF)
        s_ctx = _dot_nt(qe_ref[kv], kx_ref[...])
        sink = jnp.concatenate([jnp.broadcast_to(sink_ref[2 * kv + g:2 * kv + g + 1, :], (tq, LANES))
                                for g in range(2)], axis=0)
        m = jnp.maximum(jnp.maximum(_row_max(s_loc), _row_max(s_ctx)), sink)
        acc = _dot(_exp2_bf16(s_loc, m), vv[:, vsl]) + _dot(_exp2_bf16(s_ctx, m), vx_ref[:, vsl])
        acc = acc + jnp.where(high, jnp.exp2(sink - m), 0.0)
        o_ref[:, vsl] = _finish_heads(acc, tq).astype(o_ref.dtype)


def _window_attention(q, k, v_ext, kx, vx_ext, sink_rows, batch, seq_len, ctx_len):
    tq = min(WIN_TQ, seq_len)
    halo = Q_BLOCK
    assert seq_len % tq == 0 and tq % halo == 0 and WINDOW <= halo
    nb = seq_len // tq
    per = tq // halo
    last = seq_len // halo - 1
    dq, dk, dv = q.shape[1], k.shape[1], v_ext.shape[1]
    assert dk == LANES and dq == 2 * LANES and dv == 2 * LANES
    cur = lambda b, i: (b * nb + i, 0)
    prv = lambda b, i: (b * (last + 1) + jnp.maximum(i * per - 1, 0), 0)
    nxt = lambda b, i: (b * (last + 1) + jnp.minimum((i + 1) * per, last), 0)
    cx = lambda b, i: (b, 0)
    return pl.pallas_call(
        functools.partial(_win_kernel, seq_len=seq_len),
        grid=(batch, nb),
        in_specs=[
            pl.BlockSpec((tq, dq), cur),
            pl.BlockSpec((halo, dk), prv), pl.BlockSpec((tq, dk), cur), pl.BlockSpec((halo, dk), nxt),
            pl.BlockSpec((halo, dv), prv), pl.BlockSpec((tq, dv), cur), pl.BlockSpec((halo, dv), nxt),
            pl.BlockSpec((ctx_len, dk), cx), pl.BlockSpec((ctx_len, dv), cx),
            pl.BlockSpec(sink_rows.shape, lambda b, i: (0, 0)),
        ],
        out_specs=pl.BlockSpec((tq, dq), cur),
        out_shape=jax.ShapeDtypeStruct((batch * seq_len, dq), BF16),
        scratch_shapes=[pltpu.VMEM((2, 2 * tq, LANES), BF16)],
        compiler_params=_params(("parallel", "parallel"), VMEM_LIMIT),
        name="window_attention",
    )(q, k, k, k, v_ext, v_ext, v_ext, kx, vx_ext, sink_rows)


def _dense_kernel(q_ref, k_ref, v_ref, sink_ref, o_ref, qe_ref):
    tq = q_ref.shape[0]
    _expand_q(q_ref, qe_ref)
    high = lax.broadcasted_iota(jnp.int32, (2 * tq, LANES), 1) >= LANES // 2
    for kv in range(2):
        vsl = slice(kv * LANES, (kv + 1) * LANES)
        s = _dot_nt(qe_ref[kv], k_ref[...])
        sink = jnp.concatenate([jnp.broadcast_to(sink_ref[2 * kv + g:2 * kv + g + 1, :], (tq, LANES))
                                for g in range(2)], axis=0)
        m = jnp.maximum(_row_max(s), sink)
        acc = _dot(_exp2_bf16(s, m), v_ref[:, vsl]) + jnp.where(high, jnp.exp2(sink - m), 0.0)
        o_ref[:, vsl] = _finish_heads(acc, tq).astype(o_ref.dtype)


def _dense_attention(q, k, v_ext, sink_rows, batch, seq_len):
    dq, dk, dv = q.shape[1], k.shape[1], v_ext.shape[1]
    assert dk == LANES and dq == 2 * LANES and dv == 2 * LANES
    blk = lambda b: (b, 0)
    return pl.pallas_call(
        _dense_kernel,
        grid=(batch,),
        in_specs=[pl.BlockSpec((seq_len, dq), blk), pl.BlockSpec((seq_len, dk), blk), pl.BlockSpec((seq_len, dv), blk),
                  pl.BlockSpec(sink_rows.shape, lambda b: (0, 0))],
        out_specs=pl.BlockSpec((seq_len, dq), blk),
        out_shape=jax.ShapeDtypeStruct((batch * seq_len, dq), BF16),
        scratch_shapes=[pltpu.VMEM((2, 2 * seq_len, LANES), BF16)],
        compiler_params=_params(("parallel",)),
        name="dense_attention",
    )(q, k, v_ext, sink_rows)


def _flash_kernel(q_ref, k_ref, v_ref, kx_ref, vx_ref, o_ref, qe_ref, m_ref, acc_ref, s_ref, mx_ref, *, tk):
    tq = q_ref.shape[0]
    nb = k_ref.shape[0] // tk
    _expand_q(q_ref, qe_ref)
    m_ref[...] = jnp.full_like(m_ref, NEG_INF)
    acc_ref[...] = jnp.zeros_like(acc_ref)

    def update(kv, s, row_max, vblk):
        m_old = m_ref[kv]
        m_new = jnp.maximum(m_old, row_max)
        alpha = jnp.exp2(m_old - m_new)
        acc_ref[kv] = alpha * acc_ref[kv] + _dot(_exp2_bf16(s, m_new), vblk)
        m_ref[kv] = m_new

    for kv in range(2):
        s = _dot_nt(qe_ref[kv], kx_ref[...])
        update(kv, s, _row_max(s), vx_ref[:, kv * LANES:(kv + 1) * LANES])

    def scores(j, slot):
        kblk = k_ref[pl.ds(pl.multiple_of(j * tk, tk), tk), :]
        for kv in range(2):
            s = _dot_nt(qe_ref[kv], kblk)
            s_ref[slot, kv] = s
            mx_ref[slot, kv] = jnp.broadcast_to(_row_max(s), mx_ref.shape[2:])

    def consume(j, slot):
        off = pl.multiple_of(j * tk, tk)
        for kv in range(2):
            update(kv, s_ref[slot, kv], mx_ref[slot, kv], v_ref[pl.ds(off, tk), kv * LANES:(kv + 1) * LANES])

    scores(0, 0)

    def body(jj, carry):
        j = 2 * jj
        scores(j + 1, 1)
        consume(j, 0)
        scores(j + 2, 0)
        consume(j + 1, 1)
        return carry

    lax.fori_loop(0, nb // 2 - 1, body, 0)
    scores(nb - 1, 1)
    consume(nb - 2, 0)
    consume(nb - 1, 1)
    for kv in range(2):
        o_ref[:, kv * LANES:(kv + 1) * LANES] = _finish_heads(acc_ref[kv], tq).astype(o_ref.dtype)


def _global_attention(q, k, v_ext, kx, vx_ext, batch, seq_len, ctx_len):
    tq = min(FLASH_TQ, seq_len)
    tk = min(FLASH_TK, seq_len // 2)
    assert seq_len % tq == 0 and seq_len % (2 * tk) == 0
    nq = seq_len // tq
    dq, dk, dv = q.shape[1], k.shape[1], v_ext.shape[1]
    assert dk == LANES and dq == 2 * LANES and dv == 2 * LANES
    qmap = lambda b, i: (b * nq + i, 0)
    whole = lambda b, i: (b, 0)
    return pl.pallas_call(
        functools.partial(_flash_kernel, tk=tk),
        grid=(batch, nq),
        in_specs=[
            pl.BlockSpec((tq, dq), qmap),
            pl.BlockSpec((seq_len, dk), whole, pipeline_mode=pl.Buffered(1)),
            pl.BlockSpec((seq_len, dv), whole, pipeline_mode=pl.Buffered(1)),
            pl.BlockSpec((ctx_len, dk), whole), pl.BlockSpec((ctx_len, dv), whole),
        ],
        out_specs=pl.BlockSpec((tq, dq), qmap),
        out_shape=jax.ShapeDtypeStruct((batch * seq_len, dq), BF16),
        scratch_shapes=[
            pltpu.VMEM((2, 2 * tq, LANES), BF16),
            pltpu.VMEM((2, 2 * tq, LANES), F32),
            pltpu.VMEM((2, 2 * tq, LANES), F32),
            pltpu.VMEM((2, 2, 2 * tq, tk), F32),
            pltpu.VMEM((2, 2, 2 * tq, LANES), F32),
        ],
        compiler_params=_params(("parallel", "parallel"), VMEM_LIMIT),
        name="global_attention",
    )(q, k, v_ext, kx, vx_ext)


def _outproj_kernel(x_ref, mod_ref, x2_ref, conv_ref, z_ref, skip_ref, gate_ref, hf_ref, hb_ref,
                    yw_ref, yg_ref, w_ref, o_ref):
    y_hy = x2_ref[...] * (conv_ref[...] + z_ref[...] * skip_ref[...])
    y_lru = jax.nn.gelu(gate_ref[...]) * (hf_ref[...] + hb_ref[...])
    acc = _dot(y_hy.astype(BF16), w_ref[0])
    acc = acc + _dot(y_lru.astype(BF16), w_ref[1])
    acc = acc + _dot(yw_ref[...], w_ref[2])
    acc = acc + _dot(yg_ref[...], w_ref[3])
    o_ref[...] = x_ref[...] + mod_ref[0, 5:6, :] * acc


def _outproj(x, mod, uc, conv2, z, skip2, gate, hf, hb, yw, yg, w_out4, rows_per_group):
    rows, d = x.shape
    tm = TOKEN_TILE
    assert rows % tm == 0 and rows_per_group % tm == 0
    ch = conv2.shape[1]
    row = lambda i: (i, 0)
    return pl.pallas_call(
        _outproj_kernel,
        grid=(rows // tm,),
        in_specs=[
            pl.BlockSpec((tm, d), row),
            pl.BlockSpec((1, N_MOD, d), _group_map(rows_per_group, tm)),
            pl.BlockSpec((tm, ch), lambda i: (i, 2)),
            pl.BlockSpec((tm, ch), row),
            pl.BlockSpec((tm, ch), row),
            pl.BlockSpec((1, ch), lambda i: (0, 0)),
            pl.BlockSpec((tm, ch), row),
            pl.BlockSpec((tm, ch), row),
            pl.BlockSpec((tm, ch), row),
            pl.BlockSpec((tm, ch), row),
            pl.BlockSpec((tm, ch), row),
            pl.BlockSpec(w_out4.shape, lambda i: (0, 0, 0)),
        ],
        out_specs=pl.BlockSpec((tm, d), row),
        out_shape=jax.ShapeDtypeStruct((rows, d), F32),
        compiler_params=_params(("parallel",), VMEM_LIMIT),
        name="outproj",
    )(x, mod, uc, conv2, z, skip2, gate, hf, hb, yw, yg, w_out4)


def _rmsnorm_kernel(x_ref, g_ref, o_ref):
    x = x_ref[...]
    ms = jnp.mean(x * x, axis=-1, keepdims=True)
    o_ref[...] = x * lax.rsqrt(ms + EPS) * g_ref[...]


def _rmsnorm(x, g):
    rows, d = x.shape
    tm = TOKEN_TILE
    return pl.pallas_call(
        _rmsnorm_kernel,
        grid=(rows // tm,),
        in_specs=[pl.BlockSpec((tm, d), lambda i: (i, 0)), pl.BlockSpec((1, d), lambda i: (0, 0))],
        out_specs=pl.BlockSpec((tm, d), lambda i: (i, 0)),
        out_shape=jax.ShapeDtypeStruct((rows, d), F32),
        compiler_params=_params(("parallel",)),
        name="final_norm",
    )(x, g)


@functools.lru_cache(maxsize=None)
def _rope_angles(n):
    rows = n // GRID_W
    n_freq = HEAD_DIM // 4
    row = np.repeat(np.arange(rows, dtype=np.float32), GRID_W)[:n]
    col = np.tile(np.arange(GRID_W, dtype=np.float32), rows)
    inv = np.power(np.float32(ROPE_BASE), -(np.arange(n_freq, dtype=np.float32) / np.float32(n_freq))).astype(np.float32)
    ang = np.concatenate([row[:, None] * inv, col[:, None] * inv], axis=-1).astype(np.float32).astype(np.float64)
    return np.cos(ang).astype(np.float32), np.sin(ang).astype(np.float32)


def _rope_tables(n):
    cos, sin = (jnp.asarray(a) for a in _rope_angles(n))
    reps = LANES // HEAD_DIM
    cos_t = jnp.tile(jnp.concatenate([cos, cos], axis=-1), (1, reps))
    sin_t = jnp.tile(jnp.concatenate([-sin, sin], axis=-1), (1, reps))
    return cos_t, sin_t


def _lane_rows(vals):
    h = vals.shape[0]
    out = jnp.zeros((SUBLANES, LANES), F32)
    return out.at[:h].set(jnp.broadcast_to(vals.astype(F32)[:, None], (h, LANES)))


def kernel(x, c, ctx, c_ctx, w_mod, b_mod, norm_g, ffn1_w13, ffn1_w2, ffn2_w13, ffn2_w2, w_in, w_out,
           hy_conv_w, hy_conv_b, hy_w1, hy_b1, hy_freq, hy_w2, hy_b2, hy_w3, hy_skip,
           lru_conv_w, lru_conv_b, lru_wa, lru_ba, lru_wx, lru_bx, lru_lambda,
           win_sink, qk_gain, final_g):
    batch, n, d = x.shape
    nc = ctx.shape[1]
    depth = w_mod.shape[0]
    assert batch == 2, "the two batch entries are packed as one complex signal in the long convolution"
    d_hy = hy_skip.shape[-1]
    d_lru = lru_conv_w.shape[-1]
    n_heads_win = win_sink.shape[-1]
    d_q = n_heads_win * HEAD_DIM
    d_kv = d_q // 2
    dims = (3 * d_hy, d_lru, d_q, d_kv)
    assert d_hy == d_lru == d_q and d_hy % LANES == 0

    xl = x.reshape(batch * n, d)
    xc = ctx.reshape(batch * nc, d)

    cond = jnp.zeros((SUBLANES, d), F32).at[:batch].set(c).at[batch].set(c_ctx)
    mods = _modulation(cond, w_mod, b_mod).reshape(depth, SUBLANES, N_MOD, d)

    cos_t, sin_t = _rope_tables(n)
    hsum = jnp.asarray(np.kron(np.eye(LANES // HEAD_DIM), np.full((HEAD_DIM, HEAD_DIM), 1.0 / HEAD_DIM)), dtype=BF16)
    tabs_l = _fft_tables(n)
    tabs_c = _fft_tables(nc)
    no_sink = jnp.full((SUBLANES, LANES), NEG_INF, F32)

    for l in range(depth):
        last = l == depth - 1
        mod_l = mods[l, :batch]
        mod_c = mods[l, batch:batch + 1]
        f1 = _ffn_weights(ffn1_w13[l], ffn1_w2[l])
        f2 = _ffn_weights(ffn2_w13[l], ffn2_w2[l])
        g0, g1, g2 = (norm_g[l, i].reshape(1, d) for i in range(3))

        xl = _ffn(xl, mod_l, g0, *f1, rows_per_group=n, mod_base=0)
        xc = _ffn(xc, mod_c, g0, *f1, rows_per_group=batch * nc, mod_base=0)

        w_in_l = w_in[l].astype(BF16)
        gq = jnp.tile(qk_gain[l, 0], LANES // HEAD_DIM).reshape(1, LANES)
        gk = jnp.tile(qk_gain[l, 1], LANES // HEAD_DIM).reshape(1, LANES)
        w4 = jnp.concatenate(
            [jnp.concatenate([jnp.zeros((1, 3 * d_hy), F32), hy_conv_w[l]], axis=0), lru_conv_w[l]], axis=1)
        b4 = jnp.concatenate([hy_conv_b[l], lru_conv_b[l]]).reshape(1, -1)
        pl_ = _mixin(xl, mod_l, g1, w_in_l, gq, gk, hsum, (cos_t, sin_t), (w4, b4), n, n, dims)
        pc_ = _mixin(xc, mod_c, g1, w_in_l, gq, gk, hsum, None, None, batch * nc, nc, dims)
        uc_l, gate_l, qw_l, kw_l, vw_l, qg_l, kg_l, vg_l = pl_
        uin_c, gate_c, qw_c, kw_c, vw_c, qg_c, kg_c, vg_c = pc_
        uc_c = _shortconv(uin_c, w4, b4, nc)

        skip1 = hy_skip[l, 0].reshape(1, d_hy)
        skip2 = hy_skip[l, 1].reshape(1, d_hy)
        spec_l = _filter_spectrum(
            *_hyena_filter(n, hy_w1[l], hy_b1[l], hy_freq[l], hy_w2[l], hy_b2[l], hy_w3[l], d_hy), tabs_l)
        c1 = _long_conv(uc_l[:, :d_hy], spec_l, 0, tabs_l)
        z_l = _hyena_gate(uc_l, 1, c1, uc_l, 0, skip1)
        c2_l = _long_conv(z_l, spec_l, 1, tabs_l)

        wa = _block_diag(lru_wa[l]).astype(BF16)
        wx = _block_diag(lru_wx[l]).astype(BF16)
        lp = jnp.stack([lru_ba[l], lru_bx[l], lru_lambda[l]], axis=1)
        lru_col = 3 * d_hy // d_lru
        hf_c, hb_c, h_last = _lru(uc_c, lru_col, wa, wx, lp, jnp.zeros((batch, 2, d_lru), F32), batch, nc)
        hf_l, hb_l, _ = _lru(uc_l, lru_col, wa, wx, lp, h_last, batch, n)

        sink_rows = _lane_rows(win_sink[l] * LOG2E)
        yw_l = _window_attention(qw_l, kw_l, vw_l, kw_c, vw_c, sink_rows, batch, n, nc)

        yg_l = _global_attention(qg_l, kg_l, vg_l, kg_c, vg_c, batch, n, nc)

        w_out4 = w_out[l].astype(BF16).reshape(4, d_hy, d)
        xl = _outproj(xl, mod_l, uc_l, c2_l, z_l, skip2, gate_l, hf_l, hb_l, yw_l, yg_l, w_out4, n)

        if not last:
            spec_c = _filter_spectrum(
                *_hyena_filter(nc, hy_w1[l], hy_b1[l], hy_freq[l], hy_w2[l], hy_b2[l], hy_w3[l], d_hy), tabs_c)
            cc1 = _long_conv(uc_c[:, :d_hy], spec_c, 0, tabs_c)
            z_c = _hyena_gate(uc_c, 1, cc1, uc_c, 0, skip1)
            c2_c = _long_conv(z_c, spec_c, 1, tabs_c)
            yw_c = _dense_attention(qw_c, kw_c, vw_c, sink_rows, batch, nc)
            yg_c = _dense_attention(qg_c, kg_c, vg_c, no_sink, batch, nc)
            xc = _outproj(xc, mod_c, uc_c, c2_c, z_c, skip2, gate_c, hf_c, hb_c, yw_c, yg_c, w_out4, batch * nc)

        xl = _ffn(xl, mod_l, g2, *f2, rows_per_group=n, mod_base=6)
        if not last:
            xc = _ffn(xc, mod_c, g2, *f2, rows_per_group=batch * nc, mod_base=6)

    return _rmsnorm(xl, final_g.reshape(1, d)).reshape(batch, n, d)
```

```python
import functools
import math

import numpy as np
import jax
import jax.numpy as jnp
from jax import lax
from jax.experimental import pallas as pl
from jax.experimental.pallas import tpu as pltpu

F32 = jnp.float32
BF16 = jnp.bfloat16

HEAD_DIM = 64
GRID_W = 64
N_MOD = 9
WINDOW = 128
Q_BLOCK = 128
HY_ORDER = 2
HY_BANDS = 16
HY_MIN_DECAY = math.log(1e-2) / 1.5
HY_MAX_DECAY = math.log(1e-2) / 0.3
LRU_BLOCKS = 4
LRU_C = 8.0
ROPE_BASE = 10000.0
EPS = 1e-6
NEG_INF = -1e30
LOG2E = math.log2(math.e)
Q_SCALE = HEAD_DIM ** -0.5 * LOG2E

LANES = 128
SUBLANES = 8
VMEM_LIMIT = 56 * 1024 * 1024

TOKEN_TILE = 512
FF_CHUNK = 256
CONV_TILE = 256
SCAN_TILE = 256
FLASH_TQ = 512
FLASH_TK = 1024
WIN_TQ = 512
FFT_COL_TILE = 1024


def _params(sem, vmem=None):
    return pltpu.CompilerParams(dimension_semantics=sem, vmem_limit_bytes=vmem)


def _split_bf16(x):
    hi = x.astype(BF16)
    lo = (x - hi.astype(F32)).astype(BF16)
    return hi, lo


def _dot(a, b):
    return jnp.dot(a, b, preferred_element_type=F32)


def _dot_nt(a, b):
    return lax.dot_general(a, b, (((1,), (1,)), ((), ())), preferred_element_type=F32)


def _dot3_const(ch, cl, x):
    xh, xl = _split_bf16(x)
    return _dot(ch, xh) + (_dot(cl, xh) + _dot(ch, xl))


def _mod_kernel(c_ref, w_ref, b_ref, o_ref):
    c = c_ref[...]
    a = c * jax.nn.sigmoid(c)
    ah, al = _split_bf16(a)
    wh, wl = _split_bf16(w_ref[0])
    o_ref[0] = _dot(ah, wh) + (_dot(al, wh) + _dot(ah, wl)) + b_ref[0]


def _modulation(cond, w_mod, b_mod):
    depth, d, nd = w_mod.shape
    tn = 1152
    assert nd % tn == 0
    return pl.pallas_call(
        _mod_kernel,
        grid=(depth, nd // tn),
        in_specs=[
            pl.BlockSpec((SUBLANES, d), lambda l, j: (0, 0)),
            pl.BlockSpec((1, d, tn), lambda l, j: (l, 0, j)),
            pl.BlockSpec((1, 1, tn), lambda l, j: (l, 0, j)),
        ],
        out_specs=pl.BlockSpec((1, SUBLANES, tn), lambda l, j: (l, 0, j)),
        out_shape=jax.ShapeDtypeStruct((depth, SUBLANES, nd), F32),
        compiler_params=_params(("parallel", "parallel")),
        name="modulation",
    )(cond, w_mod, b_mod.reshape(depth, 1, nd))


def _adaln(x, g, shift, scale):
    ms = jnp.mean(x * x, axis=-1, keepdims=True)
    y = x * lax.rsqrt(ms + EPS) * g
    return y * (1.0 + scale) + shift


def _group_map(rows_per_group, tile):
    tiles = rows_per_group // tile
    return lambda i: (i // tiles, 0, 0)


def _ffn_kernel(*refs, mod_base, final_norm):
    if final_norm:
        x_ref, mod_ref, g_ref, w13_ref, w2_ref, fg_ref, o_ref, xn_ref, acc_ref = refs
    else:
        x_ref, mod_ref, g_ref, w13_ref, w2_ref, o_ref, xn_ref, acc_ref = refs
    ff = w2_ref.shape[0]
    x = x_ref[...]
    shift = mod_ref[0, mod_base:mod_base + 1, :]
    scale = mod_ref[0, mod_base + 1:mod_base + 2, :]
    gate = mod_ref[0, mod_base + 2:mod_base + 3, :]
    xn_ref[...] = _adaln(x, g_ref[...], shift, scale).astype(BF16)
    acc_ref[...] = jnp.zeros_like(acc_ref)
    for c in range(ff // FF_CHUNK):
        lo = c * FF_CHUNK
        xn = xn_ref[...]
        h = _dot(xn, w13_ref[:, lo:lo + FF_CHUNK])
        u = _dot(xn, w13_ref[:, ff + lo:ff + lo + FF_CHUNK])
        a = (h * jax.nn.sigmoid(h) * u).astype(BF16)
        acc_ref[...] += _dot(a, w2_ref[lo:lo + FF_CHUNK, :])
    y = x + 0.5 * gate * acc_ref[...]
    if final_norm:
        ms = jnp.mean(y * y, axis=-1, keepdims=True)
        y = y * lax.rsqrt(ms + EPS) * fg_ref[...]
    o_ref[...] = y


def _ffn(x, mod, g, w13, w2, rows_per_group, mod_base, final_g=None):
    rows, d = x.shape
    tm = TOKEN_TILE
    ff = w2.shape[0]
    assert rows % tm == 0 and rows_per_group % tm == 0 and ff % FF_CHUNK == 0 and w13.shape == (d, 2 * ff)
    const2 = lambda i: (0, 0)
    in_specs = [
        pl.BlockSpec((tm, d), lambda i: (i, 0)),
        pl.BlockSpec((1, N_MOD, d), _group_map(rows_per_group, tm)),
        pl.BlockSpec((1, d), const2),
        pl.BlockSpec((d, 2 * ff), const2, pipeline_mode=pl.Buffered(1)),
        pl.BlockSpec((ff, d), const2, pipeline_mode=pl.Buffered(1)),
    ]
    args = [x, mod, g, w13, w2]
    if final_g is not None:
        in_specs.append(pl.BlockSpec((1, d), const2))
        args.append(final_g)
    return pl.pallas_call(
        functools.partial(_ffn_kernel, mod_base=mod_base, final_norm=final_g is not None),
        grid=(rows // tm,),
        in_specs=in_specs,
        out_specs=pl.BlockSpec((tm, d), lambda i: (i, 0)),
        out_shape=jax.ShapeDtypeStruct((rows, d), F32),
        scratch_shapes=[pltpu.VMEM((tm, d), BF16), pltpu.VMEM((tm, d), F32)],
        compiler_params=_params(("parallel",), VMEM_LIMIT),
        name="ffn",
    )(*args)


def _ffn_weights(w13, w2):
    return w13.astype(BF16), w2.astype(BF16)


def _swap_halves(z):
    lane = lax.broadcasted_iota(jnp.int32, z.shape, 1)
    first = (lane % HEAD_DIM) < (HEAD_DIM // 2)
    return jnp.where(first, pltpu.roll(z, LANES - HEAD_DIM // 2, 1), pltpu.roll(z, HEAD_DIM // 2, 1))


def _mixin_kernel(*refs, use_rope, tiles_per_seq, d_hy3, d_lru, d_q, d_kv):
    if use_rope:
        (x_ref, mod_ref, g_ref, w_ref, gq_ref, gk_ref, hs_ref, cos_ref, sin_ref, xp_ref, xq_ref, cw_ref, cb_ref,
         uin_ref, gate_ref, qw_ref, kw_ref, vw_ref, qg_ref, kg_ref, vg_ref, ext_ref) = refs
    else:
        (x_ref, mod_ref, g_ref, w_ref, gq_ref, gk_ref, hs_ref,
         uin_ref, gate_ref, qw_ref, kw_ref, vw_ref, qg_ref, kg_ref, vg_ref) = refs
        cos_ref = sin_ref = None
    x = x_ref[...]
    shift, scale_mod = mod_ref[0, 3:4, :], mod_ref[0, 4:5, :]
    xn = _adaln(x, g_ref[...], shift, scale_mod).astype(BF16)

    def proj(lo, width):
        return _dot(xn, w_ref[:, lo:lo + width])

    def rope(z):
        if not use_rope:
            return z
        return z * cos_ref[...] + _swap_halves(z) * sin_ref[...]

    def head_norm(z, gain):
        z2 = z * z
        zh, zl = _split_bf16(z2)
        ms = _dot(zh, hs_ref[...]) + _dot(zl, hs_ref[...])
        return z * lax.rsqrt(ms + EPS) * gain

    def store_values(v_ref, v):
        assert d_kv == LANES
        low = lax.broadcasted_iota(jnp.int32, v.shape, 1) < HEAD_DIM
        v_ref[:, 0:LANES] = jnp.where(low, v, 1.0).astype(BF16)
        v_ref[:, LANES:2 * LANES] = jnp.where(low, pltpu.roll(v, HEAD_DIM, 1), 1.0).astype(BF16)

    off = 0
    d_conv = d_hy3 + d_lru
    if use_rope:
        tm = x.shape[0]
        j = pl.program_id(0) % tiles_per_seq
        keep_prev = (j > 0).astype(F32)
        keep_next = (j < tiles_per_seq - 1).astype(F32)
        halo = jnp.concatenate([xp_ref[...], xq_ref[...]], axis=0)
        hn = _adaln(halo, g_ref[...], shift, scale_mod).astype(BF16)
        hp = _dot(hn, w_ref[:, 0:d_conv])
        u = proj(off, d_conv)
        ext_ref[0:SUBLANES, :] = hp[0:SUBLANES] * keep_prev
        ext_ref[SUBLANES:SUBLANES + tm, :] = u
        ext_ref[SUBLANES + tm:, :] = hp[SUBLANES:] * keep_next
        acc = cb_ref[...] + cw_ref[2:3, :] * u
        for k in (0, 1, 3):
            acc = acc + cw_ref[k:k + 1, :] * ext_ref[SUBLANES - 2 + k:SUBLANES - 2 + k + tm, :]
        uin_ref[...] = acc
    else:
        uin_ref[...] = proj(off, d_conv)
    off += d_conv
    gate_ref[...] = proj(off, d_lru)
    off += d_lru
    scale = Q_SCALE
    z = proj(off, d_q)
    for c in range(d_q // LANES):
        qw_ref[:, c * LANES:(c + 1) * LANES] = (rope(z[:, c * LANES:(c + 1) * LANES]) * scale).astype(BF16)
    off += d_q
    z = proj(off, 2 * d_kv)
    kw_ref[...] = rope(z[:, :d_kv]).astype(BF16)
    store_values(vw_ref, z[:, d_kv:])
    off += 2 * d_kv
    z = proj(off, d_q)
    for c in range(d_q // LANES):
        zn = head_norm(z[:, c * LANES:(c + 1) * LANES], gq_ref[...])
        qg_ref[:, c * LANES:(c + 1) * LANES] = (rope(zn) * scale).astype(BF16)
    off += d_q
    z = proj(off, 2 * d_kv)
    kg_ref[...] = rope(head_norm(z[:, :d_kv], gk_ref[...])).astype(BF16)
    store_values(vg_ref, z[:, d_kv:])


def _mixin(x, mod, g, w_in, gq, gk, hsum, rope_tabs, conv, rows_per_group, seq_len, dims):
    rows, d = x.shape
    d_hy3, d_lru, d_q, d_kv = dims
    tm = TOKEN_TILE
    assert rows % tm == 0 and rows_per_group % tm == 0
    use_rope = rope_tabs is not None
    tiles_per_seq = seq_len // tm
    d_in = w_in.shape[1]
    row = lambda i: (i, 0)
    const2 = lambda i: (0, 0)
    in_specs = [
        pl.BlockSpec((tm, d), row),
        pl.BlockSpec((1, N_MOD, d), _group_map(rows_per_group, tm)),
        pl.BlockSpec((1, d), const2),
        pl.BlockSpec((d, d_in), const2, pipeline_mode=pl.Buffered(1)),
        pl.BlockSpec((1, LANES), const2),
        pl.BlockSpec((1, LANES), const2),
        pl.BlockSpec((LANES, LANES), const2),
    ]
    args = [x, mod, g, w_in, gq, gk, hsum]
    scratch = []
    if use_rope:
        assert seq_len % tm == 0
        tab = lambda i: (i % tiles_per_seq, 0)
        sub = tm // SUBLANES
        last = rows // SUBLANES - 1
        w4, b4 = conv
        in_specs += [pl.BlockSpec((tm, LANES), tab), pl.BlockSpec((tm, LANES), tab),
                     pl.BlockSpec((SUBLANES, d), lambda i: (jnp.maximum(i * sub - 1, 0), 0)),
                     pl.BlockSpec((SUBLANES, d), lambda i: (jnp.minimum((i + 1) * sub, last), 0)),
                     pl.BlockSpec(w4.shape, const2), pl.BlockSpec(b4.shape, const2)]
        args += list(rope_tabs) + [x, x, w4, b4]
        scratch = [pltpu.VMEM((tm + 2 * SUBLANES, d_hy3 + d_lru), F32)]
    widths = (d_hy3 + d_lru, d_lru, d_q, d_kv, 2 * d_kv, d_q, d_kv, 2 * d_kv)
    dtypes = (F32, F32, BF16, BF16, BF16, BF16, BF16, BF16)
    return pl.pallas_call(
        functools.partial(_mixin_kernel, use_rope=use_rope, tiles_per_seq=tiles_per_seq,
                          d_hy3=d_hy3, d_lru=d_lru, d_q=d_q, d_kv=d_kv),
        grid=(rows // tm,),
        in_specs=in_specs,
        out_specs=[pl.BlockSpec((tm, w), row) for w in widths],
        out_shape=[jax.ShapeDtypeStruct((rows, w), dt) for w, dt in zip(widths, dtypes)],
        scratch_shapes=scratch,
        compiler_params=_params(("parallel",), VMEM_LIMIT),
        name="mixin_rope" if use_rope else "mixin",
    )(*args)


def _shortconv_kernel(cur_ref, prev_ref, next_ref, w_ref, b_ref, o_ref, ext_ref, *, tiles_per_seq):
    ts = cur_ref.shape[0]
    j = pl.program_id(0) % tiles_per_seq
    keep_prev = (j > 0).astype(F32)
    keep_next = (j < tiles_per_seq - 1).astype(F32)
    ext_ref[0:SUBLANES, :] = prev_ref[...] * keep_prev
    ext_ref[SUBLANES:SUBLANES + ts, :] = cur_ref[...]
    ext_ref[SUBLANES + ts:, :] = next_ref[...] * keep_next
    acc = b_ref[...] + w_ref[2:3, :] * cur_ref[...]
    for k in (0, 1, 3):
        acc = acc + w_ref[k:k + 1, :] * ext_ref[SUBLANES - 2 + k:SUBLANES - 2 + k + ts, :]
    o_ref[...] = acc


def _shortconv(u, w4, bias, seq_len):
    rows, ch = u.shape
    ts = min(CONV_TILE, seq_len)
    assert seq_len % ts == 0 and rows % ts == 0 and ts % SUBLANES == 0
    tiles_per_seq = seq_len // ts
    sub = ts // SUBLANES
    last = rows // SUBLANES - 1
    return pl.pallas_call(
        functools.partial(_shortconv_kernel, tiles_per_seq=tiles_per_seq),
        grid=(rows // ts,),
        in_specs=[
            pl.BlockSpec((ts, ch), lambda i: (i, 0)),
            pl.BlockSpec((SUBLANES, ch), lambda i: (jnp.maximum(i * sub - 1, 0), 0)),
            pl.BlockSpec((SUBLANES, ch), lambda i: (jnp.minimum((i + 1) * sub, last), 0)),
            pl.BlockSpec((4, ch), lambda i: (0, 0)),
            pl.BlockSpec((1, ch), lambda i: (0, 0)),
        ],
        out_specs=pl.BlockSpec((ts, ch), lambda i: (i, 0)),
        out_shape=jax.ShapeDtypeStruct((rows, ch), F32),
        scratch_shapes=[pltpu.VMEM((ts + 2 * SUBLANES, ch), F32)],
        compiler_params=_params(("parallel",)),
        name="shortconv",
    )(u, u, u, w4, bias)


def _fft_split(n):
    big = 2 * n
    n2 = LANES if big >= 16 * LANES else LANES // 4
    assert big % n2 == 0 and (big // n2) % 16 == 0
    return big // n2, n2


def _np_hi_lo(m):
    m32 = jnp.asarray(np.asarray(m, dtype=np.float32))
    hi = m32.astype(BF16)
    lo = (m32 - hi.astype(F32)).astype(BF16)
    return hi, lo


@functools.lru_cache(maxsize=None)
def _fft_tables(n):
    n1, n2 = _fft_split(n)
    big = n1 * n2
    h1 = n1 // 2
    f1 = np.arange(n1)[:, None].astype(np.float64)
    s1 = np.arange(h1)[None, :].astype(np.float64)
    th = 2.0 * np.pi * f1 * s1 / n1
    c, s = np.cos(th), np.sin(th)
    w1 = np.block([[c, s], [-s, c]])
    w3 = np.block([[c.T, -s.T], [s.T, c.T]]) / big
    s1f = np.arange(n1)[None, :].astype(np.float64)
    thf = 2.0 * np.pi * f1 * s1f / n1
    w1f = np.concatenate([np.cos(thf), -np.sin(thf)], axis=0)
    f2 = np.arange(n2)[:, None].astype(np.float64)
    s2 = np.arange(n2)[None, :].astype(np.float64)
    th2 = 2.0 * np.pi * f2 * s2 / n2
    eye = np.eye(LANES // n2)
    c2, sn2 = np.kron(eye, np.cos(th2)), np.kron(eye, np.sin(th2))
    m2 = np.block([[c2, -sn2], [sn2, c2]])
    ph = 2.0 * np.pi * np.arange(n1)[:, None] * np.arange(n2)[None, :] / big
    tw = np.stack([np.cos(ph), -np.sin(ph)], axis=0)
    tw = np.tile(tw, (1, 1, LANES // n2)).astype(np.float32)
    return dict(n1=n1, n2=n2, w1=w1, w3=w3, w1f=w1f, m2=m2, m2t=m2.T, tw=tw)


def _dot3_rconst(x, ch, cl):
    xh, xl = _split_bf16(x)
    return _dot(xh, ch) + (_dot(xl, ch) + _dot(xh, cl))


def _fftconv_kernel(*refs, fwd_only, n1, groups):
    if fwd_only:
        v_ref, inv_ref, tw_ref, w1h_ref, w1l_ref, mh_ref, ml_ref, o_ref = refs
    else:
        (v_ref, sr_ref, si_ref, a_ref, sk_ref, tw_ref, w1h_ref, w1l_ref, mh_ref, ml_ref, mth_ref, mtl_ref,
         w3h_ref, w3l_ref, o_ref) = refs

    def stack(x):
        return jnp.concatenate([x[:, g * LANES:(g + 1) * LANES] for g in range(groups)], axis=0)

    def unstack(x):
        return jnp.concatenate([x[g * n1:(g + 1) * n1] for g in range(groups)], axis=1)

    a = _dot3_const(w1h_ref[...], w1l_ref[...], v_ref[...])
    twr = jnp.concatenate([tw_ref[0]] * groups, axis=0)
    twi = jnp.concatenate([tw_ref[1]] * groups, axis=0)
    ar, ai = stack(a[:n1]), stack(a[n1:])
    b = jnp.concatenate([ar * twr - ai * twi, ar * twi + ai * twr], axis=1)
    x = _dot3_rconst(b, mh_ref[...], ml_ref[...])
    xr, xi = x[:, :LANES], x[:, LANES:]
    if fwd_only:
        o_ref[0:n1, :] = unstack(xr) * inv_ref[...]
        o_ref[n1:2 * n1, :] = unstack(xi) * inv_ref[...]
        return
    kr, ki = stack(sr_ref[...]), stack(si_ref[...])
    y = jnp.concatenate([xr * kr - xi * ki, xr * ki + xi * kr], axis=1)
    c = _dot3_rconst(y, mth_ref[...], mtl_ref[...])
    cr, ci = c[:, :LANES], c[:, LANES:]
    d = jnp.concatenate([unstack(cr * twr + ci * twi), unstack(ci * twr - cr * twi)], axis=0)
    conv = _dot3_const(w3h_ref[...], w3l_ref[...], d)
    o_ref[...] = a_ref[...] * (conv + v_ref[...] * sk_ref[...])


def _fftconv(v2, spec, order, tabs, fwd_only, cols=None, v_col=0, mult=None, mult_col=0, skip=None):
    n1 = tabs["n1"]
    cols = v2.shape[1] if cols is None else cols
    tc = min(FFT_COL_TILE, cols)
    assert cols % tc == 0 and v2.shape[0] == n1 and v_col % tc == 0 and mult_col % tc == 0
    nblk = cols // tc
    v_blk, m_blk = v_col // tc, mult_col // tc
    const2 = lambda j: (0, 0)
    const3 = lambda j: (0, 0, 0)
    w1h, w1l = _np_hi_lo(tabs["w1f"] if fwd_only else tabs["w1"])
    mh, ml = _np_hi_lo(tabs["m2"])
    in_specs = [pl.BlockSpec((n1, tc), lambda j: (0, v_blk + j))]
    args = [v2]
    if fwd_only:
        in_specs.append(pl.BlockSpec((1, tc), lambda j: (0, j)))
        args.append(spec)
    else:
        in_specs += [pl.BlockSpec((n1, tc), lambda j: (0, order * nblk + j)),
                     pl.BlockSpec((n1, tc), lambda j: (1, order * nblk + j)),
                     pl.BlockSpec((n1, tc), lambda j: (0, m_blk + j)),
                     pl.BlockSpec((1, tc), lambda j: (0, j))]
        args += [spec, spec, mult, skip]
    in_specs += [pl.BlockSpec((2, n1, LANES), const3), pl.BlockSpec(w1h.shape, const2), pl.BlockSpec(w1h.shape, const2),
                 pl.BlockSpec(mh.shape, const2), pl.BlockSpec(mh.shape, const2)]
    args += [jnp.asarray(tabs["tw"]), w1h, w1l, mh, ml]
    if not fwd_only:
        mth, mtl = _np_hi_lo(tabs["m2t"])
        w3h, w3l = _np_hi_lo(tabs["w3"])
        in_specs += [pl.BlockSpec(mh.shape, const2), pl.BlockSpec(mh.shape, const2),
                     pl.BlockSpec(w3h.shape, const2), pl.BlockSpec(w3h.shape, const2)]
        args += [mth, mtl, w3h, w3l]
    out_rows = 2 * n1 if fwd_only else n1
    return pl.pallas_call(
        functools.partial(_fftconv_kernel, fwd_only=fwd_only, n1=n1, groups=tc // LANES),
        grid=(nblk,),
        in_specs=in_specs,
        out_specs=pl.BlockSpec((out_rows, tc), lambda j: (0, j)),
        out_shape=jax.ShapeDtypeStruct((out_rows, cols), F32),
        compiler_params=_params(("parallel",), VMEM_LIMIT),
        name="fft_filter" if fwd_only else "fft_conv",
    )(*args)


def _to_fft_layout(v, n1, n2):
    ch = v.shape[1]
    return v.reshape(n1, n2, ch).transpose(0, 2, 1).reshape(n1, ch * n2)


def _from_fft_layout(y, n2, ch):
    rows = y.shape[0]
    return y.reshape(rows, ch, n2).transpose(0, 2, 1).reshape(rows * n2, ch)


def _filter_spectrum(k, norm, tabs):
    inv = jnp.repeat(1.0 / norm, tabs["n2"], axis=1)
    return _fftconv(_to_fft_layout(k, tabs["n1"], tabs["n2"]), inv, 0, tabs, True)


def _hyena_mixer(u, spec, skip, tabs):
    n1, n2 = tabs["n1"], tabs["n2"]
    ch = u.shape[1] // 3
    cols = ch * n2
    u2 = _to_fft_layout(u, n1, n2)
    sk = jnp.repeat(skip, n2, axis=1)
    z2 = _fftconv(u2, spec, 0, tabs, False, cols=cols, v_col=0, mult=u2, mult_col=cols, skip=sk[0:1])
    y2 = _fftconv(z2, spec, 1, tabs, False, cols=cols, v_col=0, mult=u2, mult_col=2 * cols, skip=sk[1:2])
    return _from_fft_layout(y2, n2, ch)


FILTER_FEATS = 64


@functools.lru_cache(maxsize=None)
def _filter_features(n):
    r = np.arange(2 * n)
    pos = np.where(r < n, r, np.where(r == n, 0, 2 * n - r))
    t = np.linspace(0.0, 1.0, n, dtype=np.float32)[pos]
    w = (np.float32(2.0 * math.pi) * pos.astype(np.float32)) / np.float32(n)
    f = np.linspace(1e-4, HY_BANDS - 1, HY_BANDS, dtype=np.float32)
    fw = (f[None, :] * w[:, None]).astype(np.float32).astype(np.float64)
    z = np.zeros((2 * n, FILTER_FEATS), np.float32)
    z[:, 0] = t
    z[:, 1:1 + HY_BANDS] = np.cos(fw)
    z[:, 1 + HY_BANDS:1 + 2 * HY_BANDS] = -np.sin(fw)
    return z


def _dot3(a, b):
    ah, al = _split_bf16(a)
    bh, bl = _split_bf16(b)
    return _dot(ah, bh) + (_dot(al, bh) + _dot(ah, bl))


def _filter_kernel(z_ref, w1_ref, b1_ref, fr_ref, w2_ref, b2_ref, w3_ref, dl_ref, k_ref, sum_ref, *, n):
    i = pl.program_id(0)
    half = z_ref.shape[0]
    cols = dl_ref.shape[1]
    z = z_ref[...]
    h = jnp.sin(fr_ref[0:1, :] * (_dot3(z, w1_ref[...]) + b1_ref[...]))
    h = jnp.sin(fr_ref[1:2, :] * (_dot3(h, w2_ref[...]) + b2_ref[...]))
    k2 = _dot3(h, w3_ref[0])

    @pl.when(i == 0)
    def _():
        sum_ref[...] = jnp.zeros_like(sum_ref)

    row = 2 * i * half + lax.broadcasted_iota(jnp.int32, (half, 1), 0)
    for p in range(2):
        t = z[:, p * FILTER_FEATS:p * FILTER_FEATS + 1]
        k = k2[:, p * cols:(p + 1) * cols] * jnp.exp(-t * dl_ref[...])
        sum_ref[...] += jnp.sum(jnp.abs(k), axis=0, keepdims=True)
        k_ref[p * half:(p + 1) * half, :] = jnp.where(row + p * half == n, 0.0, k)


def _pair_diag(w):
    return jnp.kron(jnp.eye(2, dtype=w.dtype), w)


def _hyena_filter(n, w1, b1, freq, w2, b2, w3, d_hy):
    hid = w1.shape[1]
    cols = HY_ORDER * d_hy
    tile = min(TOKEN_TILE, n)
    assert n % tile == 0 and 2 * FILTER_FEATS == LANES
    tiles_per_dir = n // tile
    half = tile // 2
    z = _filter_features(n).reshape(2 * tiles_per_dir, 2, half, FILTER_FEATS)
    z = jnp.asarray(np.ascontiguousarray(z.transpose(0, 2, 1, 3)).reshape(2 * tiles_per_dir * half, LANES))
    w1p = _pair_diag(jnp.zeros((FILTER_FEATS, hid), F32).at[:w1.shape[0]].set(w1))
    w3d = w3.reshape(hid, HY_ORDER, 2, d_hy).transpose(2, 0, 1, 3).reshape(2, hid, cols)
    w3p = jnp.stack([_pair_diag(w3d[0]), _pair_diag(w3d[1])])
    deltas = jnp.abs(jnp.linspace(HY_MIN_DECAY, HY_MAX_DECAY, d_hy, dtype=F32))
    dl = jnp.tile(deltas, HY_ORDER).reshape(1, cols)
    pair = lambda v: jnp.tile(v.reshape(-1, hid), (1, 2))
    const2 = lambda i: (0, 0)
    k, sums = pl.pallas_call(
        functools.partial(_filter_kernel, n=n),
        grid=(2 * tiles_per_dir,),
        in_specs=[
            pl.BlockSpec((half, LANES), lambda i: (i, 0)),
            pl.BlockSpec((LANES, 2 * hid), const2),
            pl.BlockSpec((1, 2 * hid), const2),
            pl.BlockSpec((2, 2 * hid), const2),
            pl.BlockSpec((2 * hid, 2 * hid), const2),
            pl.BlockSpec((1, 2 * hid), const2),
            pl.BlockSpec((1, 2 * hid, 2 * cols), lambda i: (i // tiles_per_dir, 0, 0)),
            pl.BlockSpec((1, cols), const2),
        ],
        out_specs=[pl.BlockSpec((tile, cols), lambda i: (i, 0)), pl.BlockSpec((1, cols), const2)],
        out_shape=[jax.ShapeDtypeStruct((2 * n, cols), F32), jax.ShapeDtypeStruct((1, cols), F32)],
        compiler_params=_params(("arbitrary",)),
        name="hyena_filter",
    )(z, w1p, pair(b1), pair(freq), _pair_diag(w2), pair(b2), w3p, dl)
    return k, sums + EPS


def _sigmoid(x):
    return 0.5 * jnp.tanh(0.5 * x) + 0.5


def _shift_rows(x, s, fill, reverse):
    row = lax.broadcasted_iota(jnp.int32, x.shape, 1)
    if reverse:
        return jnp.where(row >= SUBLANES - s, fill, pltpu.roll(x, SUBLANES - s, 1))
    return jnp.where(row < s, fill, pltpu.roll(x, s, 1))


def _lru_kernel(uf_ref, ub_ref, wa_ref, wx_ref, p_ref, h0_ref, hf_ref, hb_ref, hl_ref, carry_ref):
    j = pl.program_id(1)
    nt = pl.num_programs(1)
    tl = uf_ref.shape[0]

    @pl.when(j == 0)
    def _():
        carry_ref[...] = h0_ref[0]

    for d, (u_ref, o_ref) in enumerate(((uf_ref, hf_ref), (ub_ref, hb_ref))):
        reverse = d == 1
        u = u_ref[...]
        ub16 = u.astype(BF16)
        r = _sigmoid(_dot(ub16, wa_ref[d]) + p_ref[d, 0:1, :])
        gi = _sigmoid(_dot(ub16, wx_ref[d]) + p_ref[d, 1:2, :])
        nlam = -p_ref[d, 2:3, :]
        softplus = jnp.maximum(nlam, 0.0) + jnp.log1p(jnp.exp(-jnp.abs(nlam)))
        log_a = -LRU_C * r * softplus
        a = jnp.exp(log_a)
        b = jnp.sqrt(-jnp.tanh(log_a) * (a * a + 1.0)) * gi * u
        a = a.reshape(tl // SUBLANES, SUBLANES, a.shape[1])
        b = b.reshape(a.shape)
        s = 1
        while s < SUBLANES:
            a_s = _shift_rows(a, s, 1.0, reverse)
            b_s = _shift_rows(b, s, 0.0, reverse)
            b = a * b_s + b
            a = a * a_s
            s *= 2
        carry = carry_ref[d:d + 1, :]
        groups = range(tl // SUBLANES)
        for g in (reversed(groups) if reverse else groups):
            hg = b[g] + a[g] * carry
            o_ref[g * SUBLANES:(g + 1) * SUBLANES, :] = hg
            carry = hg[0:1, :] if reverse else hg[SUBLANES - 1:SUBLANES, :]
        carry_ref[d:d + 1, :] = carry

    @pl.when(j == nt - 1)
    def _():
        hl_ref[0] = carry_ref[...]


def _lru(uc, col, wa, wx, p, h0, batch, seq_len):
    ch = wa.shape[-1]
    tl = min(SCAN_TILE, seq_len)
    assert seq_len % tl == 0
    nt = seq_len // tl
    rows = batch * seq_len
    fwd = lambda b, j: (b * nt + j, col)
    bwd = lambda b, j: (b * nt + nt - 1 - j, col)
    fwd_o = lambda b, j: (b * nt + j, 0)
    bwd_o = lambda b, j: (b * nt + nt - 1 - j, 0)
    const3 = lambda b, j: (0, 0, 0)
    return pl.pallas_call(
        _lru_kernel,
        grid=(batch, nt),
        in_specs=[
            pl.BlockSpec((tl, ch), fwd),
            pl.BlockSpec((tl, ch), bwd),
            pl.BlockSpec((2, ch, ch), const3),
            pl.BlockSpec((2, ch, ch), const3),
            pl.BlockSpec((2, 3, ch), const3),
            pl.BlockSpec((1, 2, ch), lambda b, j: (b, 0, 0)),
        ],
        out_specs=[
            pl.BlockSpec((tl, ch), fwd_o),
            pl.BlockSpec((tl, ch), bwd_o),
            pl.BlockSpec((1, 2, ch), lambda b, j: (b, 0, 0)),
        ],
        out_shape=[
            jax.ShapeDtypeStruct((rows, ch), F32),
            jax.ShapeDtypeStruct((rows, ch), F32),
            jax.ShapeDtypeStruct((batch, 2, ch), F32),
        ],
        scratch_shapes=[pltpu.VMEM((2, ch), F32)],
        compiler_params=_params(("parallel", "arbitrary")),
        name="rglru",
    )(uc, uc, wa, wx, p, h0)


def _block_diag(w):
    two, nb, d, _ = w.shape
    eye = jnp.eye(nb, dtype=w.dtype)
    return jnp.einsum('xnde,nm->xndme', w, eye).reshape(two, nb * d, nb * d)


def _expand_q(q_ref, qe_ref):
    tq = q_ref.shape[0]
    half = LANES // 2
    low = lax.broadcasted_iota(jnp.int32, (tq, LANES), 1) < half
    for kv in range(2):
        qf = q_ref[:, kv * LANES:(kv + 1) * LANES].astype(F32)
        qr = pltpu.roll(qf, half, 1)
        if kv == 0:
            g0, g1 = jnp.where(low, qf, 0.0), jnp.where(low, qr, 0.0)
        else:
            g0, g1 = jnp.where(low, 0.0, qr), jnp.where(low, 0.0, qf)
        qe_ref[kv, 0:tq, :] = g0.astype(BF16)
        qe_ref[kv, tq:2 * tq, :] = g1.astype(BF16)


def _row_max(s):
    m = s[:, 0:LANES]
    for c in range(1, s.shape[1] // LANES):
        m = jnp.maximum(m, s[:, c * LANES:(c + 1) * LANES])
    return jnp.max(m, axis=-1, keepdims=True)


def _exp2_bf16(s, m):
    return jnp.concatenate(
        [jnp.exp2(s[:, c * LANES:(c + 1) * LANES] - m).astype(BF16) for c in range(s.shape[1] // LANES)], axis=1)


def _finish_heads(acc, tq):
    half = LANES // 2
    o = acc / pltpu.roll(acc, half, 1)
    low = lax.broadcasted_iota(jnp.int32, (tq, LANES), 1) < half
    return jnp.where(low, o[0:tq], pltpu.roll(o[tq:2 * tq], half, 1))


def _win_kernel(q_ref, kp_ref, kc_ref, kn_ref, vp_ref, vc_ref, vn_ref, kx_ref, vx_ref, sink_ref, o_ref, qe_ref,
                *, seq_len):
    i = pl.program_id(1)
    tq = q_ref.shape[0]
    halo = kp_ref.shape[0]
    _expand_q(q_ref, qe_ref)
    kk = jnp.concatenate([kp_ref[...], kc_ref[...], kn_ref[...]], axis=0)
    vv = jnp.concatenate([vp_ref[...], vc_ref[...], vn_ref[...]], axis=0)
    n_loc = tq + 2 * halo
    row = lax.broadcasted_iota(jnp.int32, (tq, n_loc), 0)
    col = lax.broadcasted_iota(jnp.int32, (tq, n_loc), 1)
    kpos = i * tq - halo + col
    valid = (jnp.abs(col - halo - row) <= WINDOW) & (kpos >= 0) & (kpos < seq_len)
    valid = jnp.concatenate([valid, valid], axis=0)
    high = lax.broadcasted_iota(jnp.int32, (2 * tq, LANES), 1) >= LANES // 2
    for kv in range(2):
        vsl = slice(kv * LANES, (kv + 1) * LANES)
        s_loc = jnp.where(valid, _dot_nt(qe_ref[kv], kk), NEG_INF)
        s_ctx = _dot_nt(qe_ref[kv], kx_ref[...])
        sink = jnp.concatenate([jnp.broadcast_to(sink_ref[2 * kv + g:2 * kv + g + 1, :], (tq, LANES))
                                for g in range(2)], axis=0)
        m = jnp.maximum(jnp.maximum(_row_max(s_loc), _row_max(s_ctx)), sink)
        acc = _dot(_exp2_bf16(s_loc, m), vv[:, vsl]) + _dot(_exp2_bf16(s_ctx, m), vx_ref[:, vsl])
        acc = acc + jnp.where(high, jnp.exp2(sink - m), 0.0)
        o_ref[:, vsl] = _finish_heads(acc, tq).astype(o_ref.dtype)


def _window_attention(q, k, v_ext, kx, vx_ext, sink_rows, batch, seq_len, ctx_len):
    tq = min(WIN_TQ, seq_len)
    halo = Q_BLOCK
    assert seq_len % tq == 0 and tq % halo == 0 and WINDOW <= halo
    nb = seq_len // tq
    per = tq // halo
    last = seq_len // halo - 1
    dq, dk, dv = q.shape[1], k.shape[1], v_ext.shape[1]
    assert dk == LANES and dq == 2 * LANES and dv == 2 * LANES
    cur = lambda b, i: (b * nb + i, 0)
    prv = lambda b, i: (b * (last + 1) + jnp.maximum(i * per - 1, 0), 0)
    nxt = lambda b, i: (b * (last + 1) + jnp.minimum((i + 1) * per, last), 0)
    cx = lambda b, i: (b, 0)
    return pl.pallas_call(
        functools.partial(_win_kernel, seq_len=seq_len),
        grid=(batch, nb),
        in_specs=[
            pl.BlockSpec((tq, dq), cur),
            pl.BlockSpec((halo, dk), prv), pl.BlockSpec((tq, dk), cur), pl.BlockSpec((halo, dk), nxt),
            pl.BlockSpec((halo, dv), prv), pl.BlockSpec((tq, dv), cur), pl.BlockSpec((halo, dv), nxt),
            pl.BlockSpec((ctx_len, dk), cx), pl.BlockSpec((ctx_len, dv), cx),
            pl.BlockSpec(sink_rows.shape, lambda b, i: (0, 0)),
        ],
        out_specs=pl.BlockSpec((tq, dq), cur),
        out_shape=jax.ShapeDtypeStruct((batch * seq_len, dq), BF16),
        scratch_shapes=[pltpu.VMEM((2, 2 * tq, LANES), BF16)],
        compiler_params=_params(("parallel", "parallel"), VMEM_LIMIT),
        name="window_attention",
    )(q, k, k, k, v_ext, v_ext, v_ext, kx, vx_ext, sink_rows)


def _dense_kernel(q_ref, k_ref, v_ref, sink_ref, o_ref, qe_ref):
    tq = q_ref.shape[0]
    _expand_q(q_ref, qe_ref)
    high = lax.broadcasted_iota(jnp.int32, (2 * tq, LANES), 1) >= LANES // 2
    for kv in range(2):
        vsl = slice(kv * LANES, (kv + 1) * LANES)
        s = _dot_nt(qe_ref[kv], k_ref[...])
        sink = jnp.concatenate([jnp.broadcast_to(sink_ref[2 * kv + g:2 * kv + g + 1, :], (tq, LANES))
                                for g in range(2)], axis=0)
        m = jnp.maximum(_row_max(s), sink)
        acc = _dot(_exp2_bf16(s, m), v_ref[:, vsl]) + jnp.where(high, jnp.exp2(sink - m), 0.0)
        o_ref[:, vsl] = _finish_heads(acc, tq).astype(o_ref.dtype)


def _dense_attention(q, k, v_ext, sink_rows, batch, seq_len):
    dq, dk, dv = q.shape[1], k.shape[1], v_ext.shape[1]
    assert dk == LANES and dq == 2 * LANES and dv == 2 * LANES
    blk = lambda b: (b, 0)
    return pl.pallas_call(
        _dense_kernel,
        grid=(batch,),
        in_specs=[pl.BlockSpec((seq_len, dq), blk), pl.BlockSpec((seq_len, dk), blk), pl.BlockSpec((seq_len, dv), blk),
                  pl.BlockSpec(sink_rows.shape, lambda b: (0, 0))],
        out_specs=pl.BlockSpec((seq_len, dq), blk),
        out_shape=jax.ShapeDtypeStruct((batch * seq_len, dq), BF16),
        scratch_shapes=[pltpu.VMEM((2, 2 * seq_len, LANES), BF16)],
        compiler_params=_params(("parallel",)),
        name="dense_attention",
    )(q, k, v_ext, sink_rows)


def _flash_kernel(q_ref, k_ref, v_ref, kx_ref, vx_ref, o_ref, qe_ref, m_ref, acc_ref, s_ref, mx_ref, *, tk):
    tq = q_ref.shape[0]
    nb = k_ref.shape[0] // tk
    _expand_q(q_ref, qe_ref)
    m_ref[...] = jnp.full_like(m_ref, NEG_INF)
    acc_ref[...] = jnp.zeros_like(acc_ref)

    def update(kv, s, row_max, vblk):
        m_old = m_ref[kv]
        m_new = jnp.maximum(m_old, row_max)
        alpha = jnp.exp2(m_old - m_new)
        acc_ref[kv] = alpha * acc_ref[kv] + _dot(_exp2_bf16(s, m_new), vblk)
        m_ref[kv] = m_new

    for kv in range(2):
        s = _dot_nt(qe_ref[kv], kx_ref[...])
        update(kv, s, _row_max(s), vx_ref[:, kv * LANES:(kv + 1) * LANES])

    def scores(j, slot):
        kblk = k_ref[pl.ds(pl.multiple_of(j * tk, tk), tk), :]
        for kv in range(2):
            s = _dot_nt(qe_ref[kv], kblk)
            s_ref[slot, kv] = s
            mx_ref[slot, kv] = jnp.broadcast_to(_row_max(s), mx_ref.shape[2:])

    def consume(j, slot):
        off = pl.multiple_of(j * tk, tk)
        for kv in range(2):
            update(kv, s_ref[slot, kv], mx_ref[slot, kv], v_ref[pl.ds(off, tk), kv * LANES:(kv + 1) * LANES])

    scores(0, 0)

    def body(jj, carry):
        j = 2 * jj
        scores(j + 1, 1)
        consume(j, 0)
        scores(j + 2, 0)
        consume(j + 1, 1)
        return carry

    lax.fori_loop(0, nb // 2 - 1, body, 0)
    scores(nb - 1, 1)
    consume(nb - 2, 0)
    consume(nb - 1, 1)
    for kv in range(2):
        o_ref[:, kv * LANES:(kv + 1) * LANES] = _finish_heads(acc_ref[kv], tq).astype(o_ref.dtype)


def _global_attention(q, k, v_ext, kx, vx_ext, batch, seq_len, ctx_len):
    tq = min(FLASH_TQ, seq_len)
    tk = min(FLASH_TK, seq_len // 2)
    assert seq_len % tq == 0 and seq_len % (2 * tk) == 0
    nq = seq_len // tq
    dq, dk, dv = q.shape[1], k.shape[1], v_ext.shape[1]
    assert dk == LANES and dq == 2 * LANES and dv == 2 * LANES
    qmap = lambda b, i: (b * nq + i, 0)
    whole = lambda b, i: (b, 0)
    return pl.pallas_call(
        functools.partial(_flash_kernel, tk=tk),
        grid=(batch, nq),
        in_specs=[
            pl.BlockSpec((tq, dq), qmap),
            pl.BlockSpec((seq_len, dk), whole, pipeline_mode=pl.Buffered(1)),
            pl.BlockSpec((seq_len, dv), whole, pipeline_mode=pl.Buffered(1)),
            pl.BlockSpec((ctx_len, dk), whole), pl.BlockSpec((ctx_len, dv), whole),
        ],
        out_specs=pl.BlockSpec((tq, dq), qmap),
        out_shape=jax.ShapeDtypeStruct((batch * seq_len, dq), BF16),
        scratch_shapes=[
            pltpu.VMEM((2, 2 * tq, LANES), BF16),
            pltpu.VMEM((2, 2 * tq, LANES), F32),
            pltpu.VMEM((2, 2 * tq, LANES), F32),
            pltpu.VMEM((2, 2, 2 * tq, tk), F32),
            pltpu.VMEM((2, 2, 2 * tq, LANES), F32),
        ],
        compiler_params=_params(("parallel", "parallel"), VMEM_LIMIT),
        name="global_attention",
    )(q, k, v_ext, kx, vx_ext)


def _outproj_kernel(x_ref, mod_ref, yh_ref, gate_ref, hf_ref, hb_ref, yw_ref, yg_ref, w_ref, o_ref):
    y_lru = jax.nn.gelu(gate_ref[...]) * (hf_ref[...] + hb_ref[...])
    acc = _dot(yh_ref[...].astype(BF16), w_ref[0])
    acc = acc + _dot(y_lru.astype(BF16), w_ref[1])
    acc = acc + _dot(yw_ref[...], w_ref[2])
    acc = acc + _dot(yg_ref[...], w_ref[3])
    o_ref[...] = x_ref[...] + mod_ref[0, 5:6, :] * acc


def _outproj(x, mod, y_hy, gate, hf, hb, yw, yg, w_out4, rows_per_group):
    rows, d = x.shape
    tm = TOKEN_TILE
    assert rows % tm == 0 and rows_per_group % tm == 0
    ch = y_hy.shape[1]
    row = lambda i: (i, 0)
    return pl.pallas_call(
        _outproj_kernel,
        grid=(rows // tm,),
        in_specs=[
            pl.BlockSpec((tm, d), row),
            pl.BlockSpec((1, N_MOD, d), _group_map(rows_per_group, tm)),
        ] + [pl.BlockSpec((tm, ch), row)] * 6 + [
            pl.BlockSpec(w_out4.shape, lambda i: (0, 0, 0)),
        ],
        out_specs=pl.BlockSpec((tm, d), row),
        out_shape=jax.ShapeDtypeStruct((rows, d), F32),
        compiler_params=_params(("parallel",), VMEM_LIMIT),
        name="outproj",
    )(x, mod, y_hy, gate, hf, hb, yw, yg, w_out4)


@functools.lru_cache(maxsize=None)
def _rope_angles(n):
    rows = n // GRID_W
    n_freq = HEAD_DIM // 4
    row = np.repeat(np.arange(rows, dtype=np.float32), GRID_W)[:n]
    col = np.tile(np.arange(GRID_W, dtype=np.float32), rows)
    inv = np.power(np.float32(ROPE_BASE), -(np.arange(n_freq, dtype=np.float32) / np.float32(n_freq))).astype(np.float32)
    ang = np.concatenate([row[:, None] * inv, col[:, None] * inv], axis=-1).astype(np.float32).astype(np.float64)
    return np.cos(ang).astype(np.float32), np.sin(ang).astype(np.float32)


def _rope_tables(n):
    cos, sin = (jnp.asarray(a) for a in _rope_angles(n))
    reps = LANES // HEAD_DIM
    cos_t = jnp.tile(jnp.concatenate([cos, cos], axis=-1), (1, reps))
    sin_t = jnp.tile(jnp.concatenate([-sin, sin], axis=-1), (1, reps))
    return cos_t, sin_t


def _lane_rows(vals):
    h = vals.shape[0]
    out = jnp.zeros((SUBLANES, LANES), F32)
    return out.at[:h].set(jnp.broadcast_to(vals.astype(F32)[:, None], (h, LANES)))


def kernel(x, c, ctx, c_ctx, w_mod, b_mod, norm_g, ffn1_w13, ffn1_w2, ffn2_w13, ffn2_w2, w_in, w_out,
           hy_conv_w, hy_conv_b, hy_w1, hy_b1, hy_freq, hy_w2, hy_b2, hy_w3, hy_skip,
           lru_conv_w, lru_conv_b, lru_wa, lru_ba, lru_wx, lru_bx, lru_lambda,
           win_sink, qk_gain, final_g):
    batch, n, d = x.shape
    nc = ctx.shape[1]
    depth = w_mod.shape[0]
    assert batch == 2, "the two batch entries are packed as one complex signal in the long convolution"
    d_hy = hy_skip.shape[-1]
    d_lru = lru_conv_w.shape[-1]
    n_heads_win = win_sink.shape[-1]
    d_q = n_heads_win * HEAD_DIM
    d_kv = d_q // 2
    dims = (3 * d_hy, d_lru, d_q, d_kv)
    assert d_hy == d_lru == d_q and d_hy % LANES == 0

    xl = x.reshape(batch * n, d)
    xc = ctx.reshape(batch * nc, d)

    cond = jnp.zeros((SUBLANES, d), F32).at[:batch].set(c).at[batch].set(c_ctx)
    mods = _modulation(cond, w_mod, b_mod).reshape(depth, SUBLANES, N_MOD, d)

    cos_t, sin_t = _rope_tables(n)
    hsum = jnp.asarray(np.kron(np.eye(LANES // HEAD_DIM), np.full((HEAD_DIM, HEAD_DIM), 1.0 / HEAD_DIM)), dtype=BF16)
    tabs_l = _fft_tables(n)
    tabs_c = _fft_tables(nc)
    no_sink = jnp.full((SUBLANES, LANES), NEG_INF, F32)

    for l in range(depth):
        last = l == depth - 1
        mod_l = mods[l, :batch]
        mod_c = mods[l, batch:batch + 1]
        f1 = _ffn_weights(ffn1_w13[l], ffn1_w2[l])
        f2 = _ffn_weights(ffn2_w13[l], ffn2_w2[l])
        g0, g1, g2 = (norm_g[l, i].reshape(1, d) for i in range(3))

        xl = _ffn(xl, mod_l, g0, *f1, rows_per_group=n, mod_base=0)
        xc = _ffn(xc, mod_c, g0, *f1, rows_per_group=batch * nc, mod_base=0)

        w_in_l = w_in[l].astype(BF16)
        gq = jnp.tile(qk_gain[l, 0], LANES // HEAD_DIM).reshape(1, LANES)
        gk = jnp.tile(qk_gain[l, 1], LANES // HEAD_DIM).reshape(1, LANES)
        w4 = jnp.concatenate(
            [jnp.concatenate([jnp.zeros((1, 3 * d_hy), F32), hy_conv_w[l]], axis=0), lru_conv_w[l]], axis=1)
        b4 = jnp.concatenate([hy_conv_b[l], lru_conv_b[l]]).reshape(1, -1)
        pl_ = _mixin(xl, mod_l, g1, w_in_l, gq, gk, hsum, (cos_t, sin_t), (w4, b4), n, n, dims)
        pc_ = _mixin(xc, mod_c, g1, w_in_l, gq, gk, hsum, None, None, batch * nc, nc, dims)
        uc_l, gate_l, qw_l, kw_l, vw_l, qg_l, kg_l, vg_l = pl_
        uin_c, gate_c, qw_c, kw_c, vw_c, qg_c, kg_c, vg_c = pc_
        uc_c = _shortconv(uin_c, w4, b4, nc)

        spec_l = _filter_spectrum(
            *_hyena_filter(n, hy_w1[l], hy_b1[l], hy_freq[l], hy_w2[l], hy_b2[l], hy_w3[l], d_hy), tabs_l)
        yh_l = _hyena_mixer(uc_l[:, :3 * d_hy], spec_l, hy_skip[l], tabs_l)

        wa = _block_diag(lru_wa[l]).astype(BF16)
        wx = _block_diag(lru_wx[l]).astype(BF16)
        lp = jnp.stack([lru_ba[l], lru_bx[l], lru_lambda[l]], axis=1)
        lru_col = 3 * d_hy // d_lru
        hf_c, hb_c, h_last = _lru(uc_c, lru_col, wa, wx, lp, jnp.zeros((batch, 2, d_lru), F32), batch, nc)
        hf_l, hb_l, _ = _lru(uc_l, lru_col, wa, wx, lp, h_last, batch, n)

        sink_rows = _lane_rows(win_sink[l] * LOG2E)
        yw_l = _window_attention(qw_l, kw_l, vw_l, kw_c, vw_c, sink_rows, batch, n, nc)

        yg_l = _global_attention(qg_l, kg_l, vg_l, kg_c, vg_c, batch, n, nc)

        w_out4 = w_out[l].astype(BF16).reshape(4, d_hy, d)
        xl = _outproj(xl, mod_l, yh_l, gate_l, hf_l, hb_l, yw_l, yg_l, w_out4, n)

        if not last:
            spec_c = _filter_spectrum(
                *_hyena_filter(nc, hy_w1[l], hy_b1[l], hy_freq[l], hy_w2[l], hy_b2[l], hy_w3[l], d_hy), tabs_c)
            yh_c = _hyena_mixer(uc_c[:, :3 * d_hy], spec_c, hy_skip[l], tabs_c)
            yw_c = _dense_attention(qw_c, kw_c, vw_c, sink_rows, batch, nc)
            yg_c = _dense_attention(qg_c, kg_c, vg_c, no_sink, batch, nc)
            xc = _outproj(xc, mod_c, yh_c, gate_c, hf_c, hb_c, yw_c, yg_c, w_out4, batch * nc)

        xl = _ffn(xl, mod_l, g2, *f2, rows_per_group=n, mod_base=6,
                  final_g=final_g.reshape(1, d) if last else None)
        if not last:
            xc = _ffn(xc, mod_c, g2, *f2, rows_per_group=batch * nc, mod_base=6)

    return xl.reshape(batch, n, d)
```

```python
import functools
import math

import numpy as np
import jax
import jax.numpy as jnp
from jax import lax
from jax.experimental import pallas as pl
from jax.experimental.pallas import tpu as pltpu

F32 = jnp.float32
BF16 = jnp.bfloat16

HEAD_DIM = 64
GRID_W = 64
N_MOD = 9
WINDOW = 128
Q_BLOCK = 128
HY_ORDER = 2
HY_BANDS = 16
HY_MIN_DECAY = math.log(1e-2) / 1.5
HY_MAX_DECAY = math.log(1e-2) / 0.3
LRU_BLOCKS = 4
LRU_C = 8.0
ROPE_BASE = 10000.0
EPS = 1e-6
NEG_INF = -1e30
LOG2E = math.log2(math.e)
Q_SCALE = HEAD_DIM ** -0.5 * LOG2E

LANES = 128
SUBLANES = 8
VMEM_LIMIT = 56 * 1024 * 1024

TOKEN_TILE = 512
FF_CHUNK = 256
CONV_TILE = 256
SCAN_TILE = 256
FLASH_TQ = 512
FLASH_TK = 1024
WIN_TQ = 512
FFT_COL_TILE = 1024


def _params(sem, vmem=None):
    return pltpu.CompilerParams(dimension_semantics=sem, vmem_limit_bytes=vmem)


def _split_bf16(x):
    hi = x.astype(BF16)
    lo = (x - hi.astype(F32)).astype(BF16)
    return hi, lo


def _dot(a, b):
    return jnp.dot(a, b, preferred_element_type=F32)


def _dot_nt(a, b):
    return lax.dot_general(a, b, (((1,), (1,)), ((), ())), preferred_element_type=F32)


def _dot3_const(ch, cl, x):
    xh, xl = _split_bf16(x)
    return _dot(ch, xh) + (_dot(cl, xh) + _dot(ch, xl))


def _mod_kernel(c_ref, w_ref, b_ref, o_ref):
    c = c_ref[...]
    a = c * jax.nn.sigmoid(c)
    ah, al = _split_bf16(a)
    wh, wl = _split_bf16(w_ref[0])
    o_ref[0] = _dot(ah, wh) + (_dot(al, wh) + _dot(ah, wl)) + b_ref[0]


def _modulation(cond, w_mod, b_mod):
    depth, d, nd = w_mod.shape
    tn = 1152
    assert nd % tn == 0
    return pl.pallas_call(
        _mod_kernel,
        grid=(depth, nd // tn),
        in_specs=[
            pl.BlockSpec((SUBLANES, d), lambda l, j: (0, 0)),
            pl.BlockSpec((1, d, tn), lambda l, j: (l, 0, j)),
            pl.BlockSpec((1, 1, tn), lambda l, j: (l, 0, j)),
        ],
        out_specs=pl.BlockSpec((1, SUBLANES, tn), lambda l, j: (l, 0, j)),
        out_shape=jax.ShapeDtypeStruct((depth, SUBLANES, nd), F32),
        compiler_params=_params(("parallel", "parallel")),
        name="modulation",
    )(cond, w_mod, b_mod.reshape(depth, 1, nd))


def _adaln(x, g, shift, scale):
    ms = jnp.mean(x * x, axis=-1, keepdims=True)
    y = x * lax.rsqrt(ms + EPS) * g
    return y * (1.0 + scale) + shift


def _group_map(rows_per_group, tile):
    tiles = rows_per_group // tile
    return lambda i: (i // tiles, 0, 0)


def _ffn_kernel(*refs, mod_base, final_norm):
    if final_norm:
        x_ref, mod_ref, g_ref, w13_ref, w2_ref, fg_ref, o_ref, xn_ref, acc_ref = refs
    else:
        x_ref, mod_ref, g_ref, w13_ref, w2_ref, o_ref, xn_ref, acc_ref = refs
    ff = w2_ref.shape[0]
    x = x_ref[...]
    shift = mod_ref[0, mod_base:mod_base + 1, :]
    scale = mod_ref[0, mod_base + 1:mod_base + 2, :]
    gate = mod_ref[0, mod_base + 2:mod_base + 3, :]
    xn_ref[...] = _adaln(x, g_ref[...], shift, scale).astype(BF16)
    acc_ref[...] = jnp.zeros_like(acc_ref)
    for c in range(ff // FF_CHUNK):
        lo = c * FF_CHUNK
        xn = xn_ref[...]
        h = _dot(xn, w13_ref[:, lo:lo + FF_CHUNK])
        u = _dot(xn, w13_ref[:, ff + lo:ff + lo + FF_CHUNK])
        a = (h * jax.nn.sigmoid(h) * u).astype(BF16)
        acc_ref[...] += _dot(a, w2_ref[lo:lo + FF_CHUNK, :])
    y = x + 0.5 * gate * acc_ref[...]
    if final_norm:
        ms = jnp.mean(y * y, axis=-1, keepdims=True)
        y = y * lax.rsqrt(ms + EPS) * fg_ref[...]
    o_ref[...] = y


def _ffn(x, mod, g, w13, w2, rows_per_group, mod_base, final_g=None):
    rows, d = x.shape
    tm = TOKEN_TILE
    ff = w2.shape[0]
    assert rows % tm == 0 and rows_per_group % tm == 0 and ff % FF_CHUNK == 0 and w13.shape == (d, 2 * ff)
    const2 = lambda i: (0, 0)
    in_specs = [
        pl.BlockSpec((tm, d), lambda i: (i, 0)),
        pl.BlockSpec((1, N_MOD, d), _group_map(rows_per_group, tm)),
        pl.BlockSpec((1, d), const2),
        pl.BlockSpec((d, 2 * ff), const2, pipeline_mode=pl.Buffered(1)),
        pl.BlockSpec((ff, d), const2, pipeline_mode=pl.Buffered(1)),
    ]
    args = [x, mod, g, w13, w2]
    if final_g is not None:
        in_specs.append(pl.BlockSpec((1, d), const2))
        args.append(final_g)
    return pl.pallas_call(
        functools.partial(_ffn_kernel, mod_base=mod_base, final_norm=final_g is not None),
        grid=(rows // tm,),
        in_specs=in_specs,
        out_specs=pl.BlockSpec((tm, d), lambda i: (i, 0)),
        out_shape=jax.ShapeDtypeStruct((rows, d), F32),
        scratch_shapes=[pltpu.VMEM((tm, d), BF16), pltpu.VMEM((tm, d), F32)],
        compiler_params=_params(("parallel",), VMEM_LIMIT),
        name="ffn",
    )(*args)


def _ffn_weights(w13, w2):
    return w13.astype(BF16), w2.astype(BF16)


def _swap_halves(z):
    lane = lax.broadcasted_iota(jnp.int32, z.shape, 1)
    first = (lane % HEAD_DIM) < (HEAD_DIM // 2)
    return jnp.where(first, pltpu.roll(z, LANES - HEAD_DIM // 2, 1), pltpu.roll(z, HEAD_DIM // 2, 1))


def _mixin_kernel(*refs, use_rope, tiles_per_seq, d_hy3, d_lru, d_q, d_kv):
    if use_rope:
        (x_ref, mod_ref, g_ref, w_ref, gq_ref, gk_ref, hs_ref, cos_ref, sin_ref, xp_ref, xq_ref, cw_ref, cb_ref,
         uh_ref, ul_ref, gate_ref, qw_ref, kw_ref, vw_ref, qg_ref, kg_ref, vg_ref, ext_ref) = refs
    else:
        (x_ref, mod_ref, g_ref, w_ref, gq_ref, gk_ref, hs_ref,
         uin_ref, gate_ref, qw_ref, kw_ref, vw_ref, qg_ref, kg_ref, vg_ref) = refs
        cos_ref = sin_ref = None
    x = x_ref[...]
    shift, scale_mod = mod_ref[0, 3:4, :], mod_ref[0, 4:5, :]
    xn = _adaln(x, g_ref[...], shift, scale_mod).astype(BF16)

    def proj(lo, width):
        return _dot(xn, w_ref[:, lo:lo + width])

    def rope(z):
        if not use_rope:
            return z
        return z * cos_ref[...] + _swap_halves(z) * sin_ref[...]

    def head_norm(z, gain):
        z2 = z * z
        zh, zl = _split_bf16(z2)
        ms = _dot(zh, hs_ref[...]) + _dot(zl, hs_ref[...])
        return z * lax.rsqrt(ms + EPS) * gain

    def store_values(v_ref, v):
        assert d_kv == LANES
        low = lax.broadcasted_iota(jnp.int32, v.shape, 1) < HEAD_DIM
        v_ref[:, 0:LANES] = jnp.where(low, v, 1.0).astype(BF16)
        v_ref[:, LANES:2 * LANES] = jnp.where(low, pltpu.roll(v, HEAD_DIM, 1), 1.0).astype(BF16)

    off = 0
    d_conv = d_hy3 + d_lru
    if use_rope:
        tm = x.shape[0]
        j = pl.program_id(0) % tiles_per_seq
        keep_prev = (j > 0).astype(F32)
        keep_next = (j < tiles_per_seq - 1).astype(F32)
        halo = jnp.concatenate([xp_ref[...], xq_ref[...]], axis=0)
        hn = _adaln(halo, g_ref[...], shift, scale_mod).astype(BF16)
        hp = _dot(hn, w_ref[:, 0:d_conv])
        u = proj(off, d_conv)
        ext_ref[0:SUBLANES, :] = hp[0:SUBLANES] * keep_prev
        ext_ref[SUBLANES:SUBLANES + tm, :] = u
        ext_ref[SUBLANES + tm:, :] = hp[SUBLANES:] * keep_next
        acc = cb_ref[...] + cw_ref[2:3, :] * u
        for k in (0, 1, 3):
            acc = acc + cw_ref[k:k + 1, :] * ext_ref[SUBLANES - 2 + k:SUBLANES - 2 + k + tm, :]
        uh_ref[...] = acc[:, :d_hy3]
        ul_ref[...] = acc[:, d_hy3:]
    else:
        uin_ref[...] = proj(off, d_conv)
    off += d_conv
    gate_ref[...] = proj(off, d_lru)
    off += d_lru
    scale = Q_SCALE
    z = proj(off, d_q)
    for c in range(d_q // LANES):
        qw_ref[:, c * LANES:(c + 1) * LANES] = (rope(z[:, c * LANES:(c + 1) * LANES]) * scale).astype(BF16)
    off += d_q
    z = proj(off, 2 * d_kv)
    kw_ref[...] = rope(z[:, :d_kv]).astype(BF16)
    store_values(vw_ref, z[:, d_kv:])
    off += 2 * d_kv
    z = proj(off, d_q)
    for c in range(d_q // LANES):
        zn = head_norm(z[:, c * LANES:(c + 1) * LANES], gq_ref[...])
        qg_ref[:, c * LANES:(c + 1) * LANES] = (rope(zn) * scale).astype(BF16)
    off += d_q
    z = proj(off, 2 * d_kv)
    kg_ref[...] = rope(head_norm(z[:, :d_kv], gk_ref[...])).astype(BF16)
    store_values(vg_ref, z[:, d_kv:])


def _mixin(x, mod, g, w_in, gq, gk, hsum, rope_tabs, conv, rows_per_group, seq_len, dims):
    rows, d = x.shape
    d_hy3, d_lru, d_q, d_kv = dims
    tm = TOKEN_TILE
    assert rows % tm == 0 and rows_per_group % tm == 0
    use_rope = rope_tabs is not None
    tiles_per_seq = seq_len // tm
    d_in = w_in.shape[1]
    row = lambda i: (i, 0)
    const2 = lambda i: (0, 0)
    in_specs = [
        pl.BlockSpec((tm, d), row),
        pl.BlockSpec((1, N_MOD, d), _group_map(rows_per_group, tm)),
        pl.BlockSpec((1, d), const2),
        pl.BlockSpec((d, d_in), const2, pipeline_mode=pl.Buffered(1)),
        pl.BlockSpec((1, LANES), const2),
        pl.BlockSpec((1, LANES), const2),
        pl.BlockSpec((LANES, LANES), const2),
    ]
    args = [x, mod, g, w_in, gq, gk, hsum]
    scratch = []
    if use_rope:
        assert seq_len % tm == 0
        tab = lambda i: (i % tiles_per_seq, 0)
        sub = tm // SUBLANES
        last = rows // SUBLANES - 1
        w4, b4 = conv
        in_specs += [pl.BlockSpec((tm, LANES), tab), pl.BlockSpec((tm, LANES), tab),
                     pl.BlockSpec((SUBLANES, d), lambda i: (jnp.maximum(i * sub - 1, 0), 0)),
                     pl.BlockSpec((SUBLANES, d), lambda i: (jnp.minimum((i + 1) * sub, last), 0)),
                     pl.BlockSpec(w4.shape, const2), pl.BlockSpec(b4.shape, const2)]
        args += list(rope_tabs) + [x, x, w4, b4]
        scratch = [pltpu.VMEM((tm + 2 * SUBLANES, d_hy3 + d_lru), F32)]
    head = (d_hy3, d_lru) if use_rope else (d_hy3 + d_lru,)
    widths = head + (d_lru, d_q, d_kv, 2 * d_kv, d_q, d_kv, 2 * d_kv)
    dtypes = (F32,) * (len(head) + 1) + (BF16,) * 6
    return pl.pallas_call(
        functools.partial(_mixin_kernel, use_rope=use_rope, tiles_per_seq=tiles_per_seq,
                          d_hy3=d_hy3, d_lru=d_lru, d_q=d_q, d_kv=d_kv),
        grid=(rows // tm,),
        in_specs=in_specs,
        out_specs=[pl.BlockSpec((tm, w), row) for w in widths],
        out_shape=[jax.ShapeDtypeStruct((rows, w), dt) for w, dt in zip(widths, dtypes)],
        scratch_shapes=scratch,
        compiler_params=_params(("parallel",), VMEM_LIMIT),
        name="mixin_rope" if use_rope else "mixin",
    )(*args)


def _shortconv_kernel(cur_ref, prev_ref, next_ref, w_ref, b_ref, o_ref, ext_ref, *, tiles_per_seq):
    ts = cur_ref.shape[0]
    j = pl.program_id(0) % tiles_per_seq
    keep_prev = (j > 0).astype(F32)
    keep_next = (j < tiles_per_seq - 1).astype(F32)
    ext_ref[0:SUBLANES, :] = prev_ref[...] * keep_prev
    ext_ref[SUBLANES:SUBLANES + ts, :] = cur_ref[...]
    ext_ref[SUBLANES + ts:, :] = next_ref[...] * keep_next
    acc = b_ref[...] + w_ref[2:3, :] * cur_ref[...]
    for k in (0, 1, 3):
        acc = acc + w_ref[k:k + 1, :] * ext_ref[SUBLANES - 2 + k:SUBLANES - 2 + k + ts, :]
    o_ref[...] = acc


def _shortconv(u, w4, bias, seq_len):
    rows, ch = u.shape
    ts = min(CONV_TILE, seq_len)
    assert seq_len % ts == 0 and rows % ts == 0 and ts % SUBLANES == 0
    tiles_per_seq = seq_len // ts
    sub = ts // SUBLANES
    last = rows // SUBLANES - 1
    return pl.pallas_call(
        functools.partial(_shortconv_kernel, tiles_per_seq=tiles_per_seq),
        grid=(rows // ts,),
        in_specs=[
            pl.BlockSpec((ts, ch), lambda i: (i, 0)),
            pl.BlockSpec((SUBLANES, ch), lambda i: (jnp.maximum(i * sub - 1, 0), 0)),
            pl.BlockSpec((SUBLANES, ch), lambda i: (jnp.minimum((i + 1) * sub, last), 0)),
            pl.BlockSpec((4, ch), lambda i: (0, 0)),
            pl.BlockSpec((1, ch), lambda i: (0, 0)),
        ],
        out_specs=pl.BlockSpec((ts, ch), lambda i: (i, 0)),
        out_shape=jax.ShapeDtypeStruct((rows, ch), F32),
        scratch_shapes=[pltpu.VMEM((ts + 2 * SUBLANES, ch), F32)],
        compiler_params=_params(("parallel",)),
        name="shortconv",
    )(u, u, u, w4, bias)


def _fft_split(n):
    big = 2 * n
    n2 = LANES if big >= 16 * LANES else LANES // 4
    assert big % n2 == 0 and (big // n2) % 16 == 0
    return big // n2, n2


def _np_hi_lo(m):
    m32 = jnp.asarray(np.asarray(m, dtype=np.float32))
    hi = m32.astype(BF16)
    lo = (m32 - hi.astype(F32)).astype(BF16)
    return hi, lo


@functools.lru_cache(maxsize=None)
def _fft_tables(n):
    n1, n2 = _fft_split(n)
    big = n1 * n2
    h1 = n1 // 2
    f1 = np.arange(n1)[:, None].astype(np.float64)
    s1 = np.arange(h1)[None, :].astype(np.float64)
    th = 2.0 * np.pi * f1 * s1 / n1
    c, s = np.cos(th), np.sin(th)
    w1 = np.block([[c, s], [-s, c]])
    w3 = np.block([[c.T, -s.T], [s.T, c.T]]) / big
    s1f = np.arange(n1)[None, :].astype(np.float64)
    thf = 2.0 * np.pi * f1 * s1f / n1
    w1f = np.concatenate([np.cos(thf), -np.sin(thf)], axis=0)
    f2 = np.arange(n2)[:, None].astype(np.float64)
    s2 = np.arange(n2)[None, :].astype(np.float64)
    th2 = 2.0 * np.pi * f2 * s2 / n2
    eye = np.eye(LANES // n2)
    c2, sn2 = np.kron(eye, np.cos(th2)), np.kron(eye, np.sin(th2))
    m2 = np.block([[c2, -sn2], [sn2, c2]])
    ph = 2.0 * np.pi * np.arange(n1)[:, None] * np.arange(n2)[None, :] / big
    tw = np.stack([np.cos(ph), -np.sin(ph)], axis=0)
    tw = np.tile(tw, (1, 1, LANES // n2)).astype(np.float32)
    return dict(n1=n1, n2=n2, w1=w1, w3=w3, w1f=w1f, m2=m2, m2t=m2.T, tw=tw)


def _dot3_rconst(x, ch, cl):
    xh, xl = _split_bf16(x)
    return _dot(xh, ch) + (_dot(xl, ch) + _dot(xh, cl))


def _fftconv_kernel(*refs, fwd_only, n1, groups):
    if fwd_only:
        v_ref, inv_ref, tw_ref, w1h_ref, w1l_ref, mh_ref, ml_ref, o_ref = refs
    else:
        (v_ref, sr_ref, si_ref, a_ref, sk_ref, tw_ref, w1h_ref, w1l_ref, mh_ref, ml_ref, mth_ref, mtl_ref,
         w3h_ref, w3l_ref, o_ref) = refs

    def stack(x):
        return jnp.concatenate([x[:, g * LANES:(g + 1) * LANES] for g in range(groups)], axis=0)

    def unstack(x):
        return jnp.concatenate([x[g * n1:(g + 1) * n1] for g in range(groups)], axis=1)

    a = _dot3_const(w1h_ref[...], w1l_ref[...], v_ref[...])
    twr = jnp.concatenate([tw_ref[0]] * groups, axis=0)
    twi = jnp.concatenate([tw_ref[1]] * groups, axis=0)
    ar, ai = stack(a[:n1]), stack(a[n1:])
    b = jnp.concatenate([ar * twr - ai * twi, ar * twi + ai * twr], axis=1)
    x = _dot3_rconst(b, mh_ref[...], ml_ref[...])
    xr, xi = x[:, :LANES], x[:, LANES:]
    if fwd_only:
        o_ref[0:n1, :] = unstack(xr) * inv_ref[...]
        o_ref[n1:2 * n1, :] = unstack(xi) * inv_ref[...]
        return
    kr, ki = stack(sr_ref[...]), stack(si_ref[...])
    y = jnp.concatenate([xr * kr - xi * ki, xr * ki + xi * kr], axis=1)
    c = _dot3_rconst(y, mth_ref[...], mtl_ref[...])
    cr, ci = c[:, :LANES], c[:, LANES:]
    d = jnp.concatenate([unstack(cr * twr + ci * twi), unstack(ci * twr - cr * twi)], axis=0)
    conv = _dot3_const(w3h_ref[...], w3l_ref[...], d)
    o_ref[...] = a_ref[...] * (conv + v_ref[...] * sk_ref[...])


def _fftconv(v2, spec, order, tabs, fwd_only, cols=None, v_col=0, mult=None, mult_col=0, skip=None):
    n1 = tabs["n1"]
    cols = v2.shape[1] if cols is None else cols
    tc = min(FFT_COL_TILE, cols)
    assert cols % tc == 0 and v2.shape[0] == n1 and v_col % tc == 0 and mult_col % tc == 0
    nblk = cols // tc
    v_blk, m_blk = v_col // tc, mult_col // tc
    const2 = lambda j: (0, 0)
    const3 = lambda j: (0, 0, 0)
    w1h, w1l = _np_hi_lo(tabs["w1f"] if fwd_only else tabs["w1"])
    mh, ml = _np_hi_lo(tabs["m2"])
    in_specs = [pl.BlockSpec((n1, tc), lambda j: (0, v_blk + j))]
    args = [v2]
    if fwd_only:
        in_specs.append(pl.BlockSpec((1, tc), lambda j: (0, j)))
        args.append(spec)
    else:
        in_specs += [pl.BlockSpec((n1, tc), lambda j: (0, order * nblk + j)),
                     pl.BlockSpec((n1, tc), lambda j: (1, order * nblk + j)),
                     pl.BlockSpec((n1, tc), lambda j: (0, m_blk + j)),
                     pl.BlockSpec((1, tc), lambda j: (0, j))]
        args += [spec, spec, mult, skip]
    in_specs += [pl.BlockSpec((2, n1, LANES), const3), pl.BlockSpec(w1h.shape, const2), pl.BlockSpec(w1h.shape, const2),
                 pl.BlockSpec(mh.shape, const2), pl.BlockSpec(mh.shape, const2)]
    args += [jnp.asarray(tabs["tw"]), w1h, w1l, mh, ml]
    if not fwd_only:
        mth, mtl = _np_hi_lo(tabs["m2t"])
        w3h, w3l = _np_hi_lo(tabs["w3"])
        in_specs += [pl.BlockSpec(mh.shape, const2), pl.BlockSpec(mh.shape, const2),
                     pl.BlockSpec(w3h.shape, const2), pl.BlockSpec(w3h.shape, const2)]
        args += [mth, mtl, w3h, w3l]
    out_rows = 2 * n1 if fwd_only else n1
    return pl.pallas_call(
        functools.partial(_fftconv_kernel, fwd_only=fwd_only, n1=n1, groups=tc // LANES),
        grid=(nblk,),
        in_specs=in_specs,
        out_specs=pl.BlockSpec((out_rows, tc), lambda j: (0, j)),
        out_shape=jax.ShapeDtypeStruct((out_rows, cols), F32),
        compiler_params=_params(("parallel",), VMEM_LIMIT),
        name="fft_filter" if fwd_only else "fft_conv",
    )(*args)


def _to_fft_layout(v, n1, n2):
    ch = v.shape[1]
    return v.reshape(n1, n2, ch).transpose(0, 2, 1).reshape(n1, ch * n2)


def _from_fft_layout(y, n2, ch):
    rows = y.shape[0]
    return y.reshape(rows, ch, n2).transpose(0, 2, 1).reshape(rows * n2, ch)


def _filter_spectrum(k, norm, tabs):
    inv = jnp.repeat(1.0 / norm, tabs["n2"], axis=1)
    return _fftconv(_to_fft_layout(k, tabs["n1"], tabs["n2"]), inv, 0, tabs, True)


def _hyena_mixer(u, spec, skip, tabs):
    n1, n2 = tabs["n1"], tabs["n2"]
    ch = u.shape[1] // 3
    cols = ch * n2
    u2 = _to_fft_layout(u, n1, n2)
    sk = jnp.repeat(skip, n2, axis=1)
    z2 = _fftconv(u2, spec, 0, tabs, False, cols=cols, v_col=0, mult=u2, mult_col=cols, skip=sk[0:1])
    y2 = _fftconv(z2, spec, 1, tabs, False, cols=cols, v_col=0, mult=u2, mult_col=2 * cols, skip=sk[1:2])
    return _from_fft_layout(y2, n2, ch)


FILTER_FEATS = 64


@functools.lru_cache(maxsize=None)
def _filter_features(n):
    r = np.arange(2 * n)
    pos = np.where(r < n, r, np.where(r == n, 0, 2 * n - r))
    t = np.linspace(0.0, 1.0, n, dtype=np.float32)[pos]
    w = (np.float32(2.0 * math.pi) * pos.astype(np.float32)) / np.float32(n)
    f = np.linspace(1e-4, HY_BANDS - 1, HY_BANDS, dtype=np.float32)
    fw = (f[None, :] * w[:, None]).astype(np.float32).astype(np.float64)
    z = np.zeros((2 * n, FILTER_FEATS), np.float32)
    z[:, 0] = t
    z[:, 1:1 + HY_BANDS] = np.cos(fw)
    z[:, 1 + HY_BANDS:1 + 2 * HY_BANDS] = -np.sin(fw)
    return z


def _dot3(a, b):
    ah, al = _split_bf16(a)
    bh, bl = _split_bf16(b)
    return _dot(ah, bh) + (_dot(al, bh) + _dot(ah, bl))


def _filter_kernel(z_ref, w1_ref, b1_ref, fr_ref, w2_ref, b2_ref, w3_ref, dl_ref, k_ref, sum_ref, *, n):
    i = pl.program_id(0)
    half = z_ref.shape[0]
    cols = dl_ref.shape[1]
    z = z_ref[...]
    h = jnp.sin(fr_ref[0:1, :] * (_dot3(z, w1_ref[...]) + b1_ref[...]))
    h = jnp.sin(fr_ref[1:2, :] * (_dot3(h, w2_ref[...]) + b2_ref[...]))
    k2 = _dot3(h, w3_ref[0])

    @pl.when(i == 0)
    def _():
        sum_ref[...] = jnp.zeros_like(sum_ref)

    row = 2 * i * half + lax.broadcasted_iota(jnp.int32, (half, 1), 0)
    for p in range(2):
        t = z[:, p * FILTER_FEATS:p * FILTER_FEATS + 1]
        k = k2[:, p * cols:(p + 1) * cols] * jnp.exp(-t * dl_ref[...])
        sum_ref[...] += jnp.sum(jnp.abs(k), axis=0, keepdims=True)
        k_ref[p * half:(p + 1) * half, :] = jnp.where(row + p * half == n, 0.0, k)


def _pair_diag(w):
    return jnp.kron(jnp.eye(2, dtype=w.dtype), w)


def _hyena_filter(n, w1, b1, freq, w2, b2, w3, d_hy):
    hid = w1.shape[1]
    cols = HY_ORDER * d_hy
    tile = min(TOKEN_TILE, n)
    assert n % tile == 0 and 2 * FILTER_FEATS == LANES
    tiles_per_dir = n // tile
    half = tile // 2
    z = _filter_features(n).reshape(2 * tiles_per_dir, 2, half, FILTER_FEATS)
    z = jnp.asarray(np.ascontiguousarray(z.transpose(0, 2, 1, 3)).reshape(2 * tiles_per_dir * half, LANES))
    w1p = _pair_diag(jnp.zeros((FILTER_FEATS, hid), F32).at[:w1.shape[0]].set(w1))
    w3d = w3.reshape(hid, HY_ORDER, 2, d_hy).transpose(2, 0, 1, 3).reshape(2, hid, cols)
    w3p = jnp.stack([_pair_diag(w3d[0]), _pair_diag(w3d[1])])
    deltas = jnp.abs(jnp.linspace(HY_MIN_DECAY, HY_MAX_DECAY, d_hy, dtype=F32))
    dl = jnp.tile(deltas, HY_ORDER).reshape(1, cols)
    pair = lambda v: jnp.tile(v.reshape(-1, hid), (1, 2))
    const2 = lambda i: (0, 0)
    k, sums = pl.pallas_call(
        functools.partial(_filter_kernel, n=n),
        grid=(2 * tiles_per_dir,),
        in_specs=[
            pl.BlockSpec((half, LANES), lambda i: (i, 0)),
            pl.BlockSpec((LANES, 2 * hid), const2),
            pl.BlockSpec((1, 2 * hid), const2),
            pl.BlockSpec((2, 2 * hid), const2),
            pl.BlockSpec((2 * hid, 2 * hid), const2),
            pl.BlockSpec((1, 2 * hid), const2),
            pl.BlockSpec((1, 2 * hid, 2 * cols), lambda i: (i // tiles_per_dir, 0, 0)),
            pl.BlockSpec((1, cols), const2),
        ],
        out_specs=[pl.BlockSpec((tile, cols), lambda i: (i, 0)), pl.BlockSpec((1, cols), const2)],
        out_shape=[jax.ShapeDtypeStruct((2 * n, cols), F32), jax.ShapeDtypeStruct((1, cols), F32)],
        compiler_params=_params(("arbitrary",)),
        name="hyena_filter",
    )(z, w1p, pair(b1), pair(freq), _pair_diag(w2), pair(b2), w3p, dl)
    return k, sums + EPS


def _sigmoid(x):
    return 0.5 * jnp.tanh(0.5 * x) + 0.5


def _shift_rows(x, s, fill, reverse):
    row = lax.broadcasted_iota(jnp.int32, x.shape, 1)
    if reverse:
        return jnp.where(row >= SUBLANES - s, fill, pltpu.roll(x, SUBLANES - s, 1))
    return jnp.where(row < s, fill, pltpu.roll(x, s, 1))


def _lru_kernel(uf_ref, ub_ref, wa_ref, wx_ref, p_ref, h0_ref, hf_ref, hb_ref, hl_ref, carry_ref):
    j = pl.program_id(1)
    nt = pl.num_programs(1)
    tl = uf_ref.shape[0]

    @pl.when(j == 0)
    def _():
        carry_ref[...] = h0_ref[0]

    for d, (u_ref, o_ref) in enumerate(((uf_ref, hf_ref), (ub_ref, hb_ref))):
        reverse = d == 1
        u = u_ref[...]
        ub16 = u.astype(BF16)
        r = _sigmoid(_dot(ub16, wa_ref[d]) + p_ref[d, 0:1, :])
        gi = _sigmoid(_dot(ub16, wx_ref[d]) + p_ref[d, 1:2, :])
        nlam = -p_ref[d, 2:3, :]
        softplus = jnp.maximum(nlam, 0.0) + jnp.log1p(jnp.exp(-jnp.abs(nlam)))
        log_a = -LRU_C * r * softplus
        a = jnp.exp(log_a)
        b = jnp.sqrt(-jnp.tanh(log_a) * (a * a + 1.0)) * gi * u
        a = a.reshape(tl // SUBLANES, SUBLANES, a.shape[1])
        b = b.reshape(a.shape)
        s = 1
        while s < SUBLANES:
            a_s = _shift_rows(a, s, 1.0, reverse)
            b_s = _shift_rows(b, s, 0.0, reverse)
            b = a * b_s + b
            a = a * a_s
            s *= 2
        carry = carry_ref[d:d + 1, :]
        groups = range(tl // SUBLANES)
        for g in (reversed(groups) if reverse else groups):
            hg = b[g] + a[g] * carry
            o_ref[g * SUBLANES:(g + 1) * SUBLANES, :] = hg
            carry = hg[0:1, :] if reverse else hg[SUBLANES - 1:SUBLANES, :]
        carry_ref[d:d + 1, :] = carry

    @pl.when(j == nt - 1)
    def _():
        hl_ref[0] = carry_ref[...]


def _lru(uc, col, wa, wx, p, h0, batch, seq_len):
    ch = wa.shape[-1]
    tl = min(SCAN_TILE, seq_len)
    assert seq_len % tl == 0
    nt = seq_len // tl
    rows = batch * seq_len
    fwd = lambda b, j: (b * nt + j, col)
    bwd = lambda b, j: (b * nt + nt - 1 - j, col)
    fwd_o = lambda b, j: (b * nt + j, 0)
    bwd_o = lambda b, j: (b * nt + nt - 1 - j, 0)
    const3 = lambda b, j: (0, 0, 0)
    return pl.pallas_call(
        _lru_kernel,
        grid=(batch, nt),
        in_specs=[
            pl.BlockSpec((tl, ch), fwd),
            pl.BlockSpec((tl, ch), bwd),
            pl.BlockSpec((2, ch, ch), const3),
            pl.BlockSpec((2, ch, ch), const3),
            pl.BlockSpec((2, 3, ch), const3),
            pl.BlockSpec((1, 2, ch), lambda b, j: (b, 0, 0)),
        ],
        out_specs=[
            pl.BlockSpec((tl, ch), fwd_o),
            pl.BlockSpec((tl, ch), bwd_o),
            pl.BlockSpec((1, 2, ch), lambda b, j: (b, 0, 0)),
        ],
        out_shape=[
            jax.ShapeDtypeStruct((rows, ch), F32),
            jax.ShapeDtypeStruct((rows, ch), F32),
            jax.ShapeDtypeStruct((batch, 2, ch), F32),
        ],
        scratch_shapes=[pltpu.VMEM((2, ch), F32)],
        compiler_params=_params(("parallel", "arbitrary")),
        name="rglru",
    )(uc, uc, wa, wx, p, h0)


def _block_diag(w):
    two, nb, d, _ = w.shape
    eye = jnp.eye(nb, dtype=w.dtype)
    return jnp.einsum('xnde,nm->xndme', w, eye).reshape(two, nb * d, nb * d)


def _expand_q(q_ref, qe_ref):
    tq = q_ref.shape[0]
    half = LANES // 2
    low = lax.broadcasted_iota(jnp.int32, (tq, LANES), 1) < half
    for kv in range(2):
        qf = q_ref[:, kv * LANES:(kv + 1) * LANES].astype(F32)
        qr = pltpu.roll(qf, half, 1)
        if kv == 0:
            g0, g1 = jnp.where(low, qf, 0.0), jnp.where(low, qr, 0.0)
        else:
            g0, g1 = jnp.where(low, 0.0, qr), jnp.where(low, 0.0, qf)
        qe_ref[kv, 0:tq, :] = g0.astype(BF16)
        qe_ref[kv, tq:2 * tq, :] = g1.astype(BF16)


def _row_max(s):
    m = s[:, 0:LANES]
    for c in range(1, s.shape[1] // LANES):
        m = jnp.maximum(m, s[:, c * LANES:(c + 1) * LANES])
    return jnp.max(m, axis=-1, keepdims=True)


def _exp2_bf16(s, m):
    return jnp.concatenate(
        [jnp.exp2(s[:, c * LANES:(c + 1) * LANES] - m).astype(BF16) for c in range(s.shape[1] // LANES)], axis=1)


def _finish_heads(acc, tq):
    half = LANES // 2
    o = acc / pltpu.roll(acc, half, 1)
    low = lax.broadcasted_iota(jnp.int32, (tq, LANES), 1) < half
    return jnp.where(low, o[0:tq], pltpu.roll(o[tq:2 * tq], half, 1))


def _win_kernel(q_ref, kp_ref, kc_ref, kn_ref, vp_ref, vc_ref, vn_ref, kx_ref, vx_ref, sink_ref, o_ref, qe_ref,
                *, seq_len):
    i = pl.program_id(1)
    tq = q_ref.shape[0]
    halo = kp_ref.shape[0]
    _expand_q(q_ref, qe_ref)
    kk = jnp.concatenate([kp_ref[...], kc_ref[...], kn_ref[...]], axis=0)
    vv = jnp.concatenate([vp_ref[...], vc_ref[...], vn_ref[...]], axis=0)
    n_loc = tq + 2 * halo
    row = lax.broadcasted_iota(jnp.int32, (tq, n_loc), 0)
    col = lax.broadcasted_iota(jnp.int32, (tq, n_loc), 1)
    kpos = i * tq - halo + col
    valid = (jnp.abs(col - halo - row) <= WINDOW) & (kpos >= 0) & (kpos < seq_len)
    valid = jnp.concatenate([valid, valid], axis=0)
    high = lax.broadcasted_iota(jnp.int32, (2 * tq, LANES), 1) >= LANES // 2
    for kv in range(2):
        vsl = slice(kv * LANES, (kv + 1) * LANES)
        s_loc = jnp.where(valid, _dot_nt(qe_ref[kv], kk), NEG_INF)
        s_ctx = _dot_nt(qe_ref[kv], kx_ref[...])
        sink = jnp.concatenate([jnp.broadcast_to(sink_ref[2 * kv + g:2 * kv + g + 1, :], (tq, LANES))
                                for g in range(2)], axis=0)
        m = jnp.maximum(jnp.maximum(_row_max(s_loc), _row_max(s_ctx)), sink)
        acc = _dot(_exp2_bf16(s_loc, m), vv[:, vsl]) + _dot(_exp2_bf16(s_ctx, m), vx_ref[:, vsl])
        acc = acc + jnp.where(high, jnp.exp2(sink - m), 0.0)
        o_ref[:, vsl] = _finish_heads(acc, tq).astype(o_ref.dtype)


def _window_attention(q, k, v_ext, kx, vx_ext, sink_rows, batch, seq_len, ctx_len):
    tq = min(WIN_TQ, seq_len)
    halo = Q_BLOCK
    assert seq_len % tq == 0 and tq % halo == 0 and WINDOW <= halo
    nb = seq_len // tq
    per = tq // halo
    last = seq_len // halo - 1
    dq, dk, dv = q.shape[1], k.shape[1], v_ext.shape[1]
    assert dk == LANES and dq == 2 * LANES and dv == 2 * LANES
    cur = lambda b, i: (b * nb + i, 0)
    prv = lambda b, i: (b * (last + 1) + jnp.maximum(i * per - 1, 0), 0)
    nxt = lambda b, i: (b * (last + 1) + jnp.minimum((i + 1) * per, last), 0)
    cx = lambda b, i: (b, 0)
    return pl.pallas_call(
        functools.partial(_win_kernel, seq_len=seq_len),
        grid=(batch, nb),
        in_specs=[
            pl.BlockSpec((tq, dq), cur),
            pl.BlockSpec((halo, dk), prv), pl.BlockSpec((tq, dk), cur), pl.BlockSpec((halo, dk), nxt),
            pl.BlockSpec((halo, dv), prv), pl.BlockSpec((tq, dv), cur), pl.BlockSpec((halo, dv), nxt),
            pl.BlockSpec((ctx_len, dk), cx), pl.BlockSpec((ctx_len, dv), cx),
            pl.BlockSpec(sink_rows.shape, lambda b, i: (0, 0)),
        ],
        out_specs=pl.BlockSpec((tq, dq), cur),
        out_shape=jax.ShapeDtypeStruct((batch * seq_len, dq), BF16),
        scratch_shapes=[pltpu.VMEM((2, 2 * tq, LANES), BF16)],
        compiler_params=_params(("parallel", "parallel"), VMEM_LIMIT),
        name="window_attention",
    )(q, k, k, k, v_ext, v_ext, v_ext, kx, vx_ext, sink_rows)


def _dense_kernel(q_ref, k_ref, v_ref, sink_ref, o_ref, qe_ref):
    tq = q_ref.shape[0]
    _expand_q(q_ref, qe_ref)
    high = lax.broadcasted_iota(jnp.int32, (2 * tq, LANES), 1) >= LANES // 2
    for kv in range(2):
        vsl = slice(kv * LANES, (kv + 1) * LANES)
        s = _dot_nt(qe_ref[kv], k_ref[...])
        sink = jnp.concatenate([jnp.broadcast_to(sink_ref[2 * kv + g:2 * kv + g + 1, :], (tq, LANES))
                                for g in range(2)], axis=0)
        m = jnp.maximum(_row_max(s), sink)
        acc = _dot(_exp2_bf16(s, m), v_ref[:, vsl]) + jnp.where(high, jnp.exp2(sink - m), 0.0)
        o_ref[:, vsl] = _finish_heads(acc, tq).astype(o_ref.dtype)


def _dense_attention(q, k, v_ext, sink_rows, batch, seq_len):
    dq, dk, dv = q.shape[1], k.shape[1], v_ext.shape[1]
    assert dk == LANES and dq == 2 * LANES and dv == 2 * LANES
    blk = lambda b: (b, 0)
    return pl.pallas_call(
        _dense_kernel,
        grid=(batch,),
        in_specs=[pl.BlockSpec((seq_len, dq), blk), pl.BlockSpec((seq_len, dk), blk), pl.BlockSpec((seq_len, dv), blk),
                  pl.BlockSpec(sink_rows.shape, lambda b: (0, 0))],
        out_specs=pl.BlockSpec((seq_len, dq), blk),
        out_shape=jax.ShapeDtypeStruct((batch * seq_len, dq), BF16),
        scratch_shapes=[pltpu.VMEM((2, 2 * seq_len, LANES), BF16)],
        compiler_params=_params(("parallel",)),
        name="dense_attention",
    )(q, k, v_ext, sink_rows)


def _flash_kernel(q_ref, k_ref, v_ref, kx_ref, vx_ref, o_ref, qe_ref, m_ref, acc_ref, s_ref, mx_ref,
                  sx_ref, mxx_ref, *, tk):
    tq = q_ref.shape[0]
    nb = k_ref.shape[0] // tk
    _expand_q(q_ref, qe_ref)
    m_ref[...] = jnp.full_like(m_ref, NEG_INF)
    acc_ref[...] = jnp.zeros_like(acc_ref)

    def update(kv, s, row_max, vblk):
        m_old = m_ref[kv]
        m_new = jnp.maximum(m_old, row_max)
        alpha = jnp.exp2(m_old - m_new)
        acc_ref[kv] = alpha * acc_ref[kv] + _dot(_exp2_bf16(s, m_new), vblk)
        m_ref[kv] = m_new

    for kv in range(2):
        s = _dot_nt(qe_ref[kv], kx_ref[...])
        sx_ref[kv] = s
        mxx_ref[kv] = jnp.broadcast_to(_row_max(s), mxx_ref.shape[1:])
    def scores(j, slot):
        kblk = k_ref[pl.ds(pl.multiple_of(j * tk, tk), tk), :]
        for kv in range(2):
            s = _dot_nt(qe_ref[kv], kblk)
            s_ref[slot, kv] = s
            mx_ref[slot, kv] = jnp.broadcast_to(_row_max(s), mx_ref.shape[2:])

    def consume(j, slot):
        off = pl.multiple_of(j * tk, tk)
        for kv in range(2):
            update(kv, s_ref[slot, kv], mx_ref[slot, kv], v_ref[pl.ds(off, tk), kv * LANES:(kv + 1) * LANES])

    scores(0, 0)
    for kv in range(2):
        update(kv, sx_ref[kv], mxx_ref[kv], vx_ref[:, kv * LANES:(kv + 1) * LANES])

    def body(jj, carry):
        j = 2 * jj
        scores(j + 1, 1)
        consume(j, 0)
        scores(j + 2, 0)
        consume(j + 1, 1)
        return carry

    lax.fori_loop(0, nb // 2 - 1, body, 0)
    scores(nb - 1, 1)
    consume(nb - 2, 0)
    consume(nb - 1, 1)
    for kv in range(2):
        o_ref[:, kv * LANES:(kv + 1) * LANES] = _finish_heads(acc_ref[kv], tq).astype(o_ref.dtype)


def _global_attention(q, k, v_ext, kx, vx_ext, batch, seq_len, ctx_len):
    tq = min(FLASH_TQ, seq_len)
    tk = min(FLASH_TK, seq_len // 2)
    assert seq_len % tq == 0 and seq_len % (2 * tk) == 0
    nq = seq_len // tq
    dq, dk, dv = q.shape[1], k.shape[1], v_ext.shape[1]
    assert dk == LANES and dq == 2 * LANES and dv == 2 * LANES
    qmap = lambda b, i: (b * nq + i, 0)
    whole = lambda b, i: (b, 0)
    return pl.pallas_call(
        functools.partial(_flash_kernel, tk=tk),
        grid=(batch, nq),
        in_specs=[
            pl.BlockSpec((tq, dq), qmap),
            pl.BlockSpec((seq_len, dk), whole, pipeline_mode=pl.Buffered(1)),
            pl.BlockSpec((seq_len, dv), whole, pipeline_mode=pl.Buffered(1)),
            pl.BlockSpec((ctx_len, dk), whole), pl.BlockSpec((ctx_len, dv), whole),
        ],
        out_specs=pl.BlockSpec((tq, dq), qmap),
        out_shape=jax.ShapeDtypeStruct((batch * seq_len, dq), BF16),
        scratch_shapes=[
            pltpu.VMEM((2, 2 * tq, LANES), BF16),
            pltpu.VMEM((2, 2 * tq, LANES), F32),
            pltpu.VMEM((2, 2 * tq, LANES), F32),
            pltpu.VMEM((2, 2, 2 * tq, tk), F32),
            pltpu.VMEM((2, 2, 2 * tq, LANES), F32),
            pltpu.VMEM((2, 2 * tq, ctx_len), F32),
            pltpu.VMEM((2, 2 * tq, LANES), F32),
        ],
        compiler_params=_params(("parallel", "parallel"), VMEM_LIMIT),
        name="global_attention",
    )(q, k, v_ext, kx, vx_ext)


def _outproj_kernel(x_ref, mod_ref, yh_ref, gate_ref, hf_ref, hb_ref, yw_ref, yg_ref, w_ref, o_ref):
    y_lru = jax.nn.gelu(gate_ref[...]) * (hf_ref[...] + hb_ref[...])
    acc = _dot(yh_ref[...].astype(BF16), w_ref[0])
    acc = acc + _dot(y_lru.astype(BF16), w_ref[1])
    acc = acc + _dot(yw_ref[...], w_ref[2])
    acc = acc + _dot(yg_ref[...], w_ref[3])
    o_ref[...] = x_ref[...] + mod_ref[0, 5:6, :] * acc


def _outproj(x, mod, y_hy, gate, hf, hb, yw, yg, w_out4, rows_per_group):
    rows, d = x.shape
    tm = TOKEN_TILE
    assert rows % tm == 0 and rows_per_group % tm == 0
    ch = y_hy.shape[1]
    row = lambda i: (i, 0)
    return pl.pallas_call(
        _outproj_kernel,
        grid=(rows // tm,),
        in_specs=[
            pl.BlockSpec((tm, d), row),
            pl.BlockSpec((1, N_MOD, d), _group_map(rows_per_group, tm)),
        ] + [pl.BlockSpec((tm, ch), row)] * 6 + [
            pl.BlockSpec(w_out4.shape, lambda i: (0, 0, 0)),
        ],
        out_specs=pl.BlockSpec((tm, d), row),
        out_shape=jax.ShapeDtypeStruct((rows, d), F32),
        compiler_params=_params(("parallel",), VMEM_LIMIT),
        name="outproj",
    )(x, mod, y_hy, gate, hf, hb, yw, yg, w_out4)


@functools.lru_cache(maxsize=None)
def _rope_angles(n):
    rows = n // GRID_W
    n_freq = HEAD_DIM // 4
    row = np.repeat(np.arange(rows, dtype=np.float32), GRID_W)[:n]
    col = np.tile(np.arange(GRID_W, dtype=np.float32), rows)
    inv = np.power(np.float32(ROPE_BASE), -(np.arange(n_freq, dtype=np.float32) / np.float32(n_freq))).astype(np.float32)
    ang = np.concatenate([row[:, None] * inv, col[:, None] * inv], axis=-1).astype(np.float32).astype(np.float64)
    return np.cos(ang).astype(np.float32), np.sin(ang).astype(np.float32)


def _rope_tables(n):
    cos, sin = (jnp.asarray(a) for a in _rope_angles(n))
    reps = LANES // HEAD_DIM
    cos_t = jnp.tile(jnp.concatenate([cos, cos], axis=-1), (1, reps))
    sin_t = jnp.tile(jnp.concatenate([-sin, sin], axis=-1), (1, reps))
    return cos_t, sin_t


def _lane_rows(vals):
    h = vals.shape[0]
    out = jnp.zeros((SUBLANES, LANES), F32)
    return out.at[:h].set(jnp.broadcast_to(vals.astype(F32)[:, None], (h, LANES)))


def kernel(x, c, ctx, c_ctx, w_mod, b_mod, norm_g, ffn1_w13, ffn1_w2, ffn2_w13, ffn2_w2, w_in, w_out,
           hy_conv_w, hy_conv_b, hy_w1, hy_b1, hy_freq, hy_w2, hy_b2, hy_w3, hy_skip,
           lru_conv_w, lru_conv_b, lru_wa, lru_ba, lru_wx, lru_bx, lru_lambda,
           win_sink, qk_gain, final_g):
    batch, n, d = x.shape
    nc = ctx.shape[1]
    depth = w_mod.shape[0]
    assert batch == 2, "the two batch entries are packed as one complex signal in the long convolution"
    d_hy = hy_skip.shape[-1]
    d_lru = lru_conv_w.shape[-1]
    n_heads_win = win_sink.shape[-1]
    d_q = n_heads_win * HEAD_DIM
    d_kv = d_q // 2
    dims = (3 * d_hy, d_lru, d_q, d_kv)
    assert d_hy == d_lru == d_q and d_hy % LANES == 0

    xl = x.reshape(batch * n, d)
    xc = ctx.reshape(batch * nc, d)

    cond = jnp.zeros((SUBLANES, d), F32).at[:batch].set(c).at[batch].set(c_ctx)
    mods = _modulation(cond, w_mod, b_mod).reshape(depth, SUBLANES, N_MOD, d)

    cos_t, sin_t = _rope_tables(n)
    hsum = jnp.asarray(np.kron(np.eye(LANES // HEAD_DIM), np.full((HEAD_DIM, HEAD_DIM), 1.0 / HEAD_DIM)), dtype=BF16)
    tabs_l = _fft_tables(n)
    tabs_c = _fft_tables(nc)
    no_sink = jnp.full((SUBLANES, LANES), NEG_INF, F32)

    for l in range(depth):
        last = l == depth - 1
        mod_l = mods[l, :batch]
        mod_c = mods[l, batch:batch + 1]
        f1 = _ffn_weights(ffn1_w13[l], ffn1_w2[l])
        f2 = _ffn_weights(ffn2_w13[l], ffn2_w2[l])
        g0, g1, g2 = (norm_g[l, i].reshape(1, d) for i in range(3))

        xl = _ffn(xl, mod_l, g0, *f1, rows_per_group=n, mod_base=0)
        xc = _ffn(xc, mod_c, g0, *f1, rows_per_group=batch * nc, mod_base=0)

        w_in_l = w_in[l].astype(BF16)
        gq = jnp.tile(qk_gain[l, 0], LANES // HEAD_DIM).reshape(1, LANES)
        gk = jnp.tile(qk_gain[l, 1], LANES // HEAD_DIM).reshape(1, LANES)
        w4 = jnp.concatenate(
            [jnp.concatenate([jnp.zeros((1, 3 * d_hy), F32), hy_conv_w[l]], axis=0), lru_conv_w[l]], axis=1)
        b4 = jnp.concatenate([hy_conv_b[l], lru_conv_b[l]]).reshape(1, -1)
        pl_ = _mixin(xl, mod_l, g1, w_in_l, gq, gk, hsum, (cos_t, sin_t), (w4, b4), n, n, dims)
        pc_ = _mixin(xc, mod_c, g1, w_in_l, gq, gk, hsum, None, None, batch * nc, nc, dims)
        uh_l, ul_l, gate_l, qw_l, kw_l, vw_l, qg_l, kg_l, vg_l = pl_
        uin_c, gate_c, qw_c, kw_c, vw_c, qg_c, kg_c, vg_c = pc_
        uc_c = _shortconv(uin_c, w4, b4, nc)

        spec_l = _filter_spectrum(
            *_hyena_filter(n, hy_w1[l], hy_b1[l], hy_freq[l], hy_w2[l], hy_b2[l], hy_w3[l], d_hy), tabs_l)
        yh_l = _hyena_mixer(uh_l, spec_l, hy_skip[l], tabs_l)

        wa = _block_diag(lru_wa[l]).astype(BF16)
        wx = _block_diag(lru_wx[l]).astype(BF16)
        lp = jnp.stack([lru_ba[l], lru_bx[l], lru_lambda[l]], axis=1)
        lru_col = 3 * d_hy // d_lru
        hf_c, hb_c, h_last = _lru(uc_c, lru_col, wa, wx, lp, jnp.zeros((batch, 2, d_lru), F32), batch, nc)
        hf_l, hb_l, _ = _lru(ul_l, 0, wa, wx, lp, h_last, batch, n)

        sink_rows = _lane_rows(win_sink[l] * LOG2E)
        yw_l = _window_attention(qw_l, kw_l, vw_l, kw_c, vw_c, sink_rows, batch, n, nc)

        yg_l = _global_attention(qg_l, kg_l, vg_l, kg_c, vg_c, batch, n, nc)

        w_out4 = w_out[l].astype(BF16).reshape(4, d_hy, d)
        xl = _outproj(xl, mod_l, yh_l, gate_l, hf_l, hb_l, yw_l, yg_l, w_out4, n)

        if not last:
            spec_c = _filter_spectrum(
                *_hyena_filter(nc, hy_w1[l], hy_b1[l], hy_freq[l], hy_w2[l], hy_b2[l], hy_w3[l], d_hy), tabs_c)
            yh_c = _hyena_mixer(uc_c[:, :3 * d_hy], spec_c, hy_skip[l], tabs_c)
            yw_c = _dense_attention(qw_c, kw_c, vw_c, sink_rows, batch, nc)
            yg_c = _dense_attention(qg_c, kg_c, vg_c, no_sink, batch, nc)
            xc = _outproj(xc, mod_c, yh_c, gate_c, hf_c, hb_c, yw_c, yg_c, w_out4, batch * nc)

        xl = _ffn(xl, mod_l, g2, *f2, rows_per_group=n, mod_base=6,
                  final_g=final_g.reshape(1, d) if last else None)
        if not last:
            xc = _ffn(xc, mod_c, g2, *f2, rows_per_group=batch * nc, mod_base=6)

    return xl.reshape(batch, n, d)
```
